```python
import math
import jax, jax.numpy as jnp
from jax import lax
import numpy as np

D_MODEL = 2048
BATCH = 4
SEQ = 8192
DEPTH = 4

D_MIX = D_MODEL
CONV_K = 4

S5_WIDTH = D_MIX // 4
S5_GROUP = 16
S5_NGROUPS = S5_WIDTH // S5_GROUP
S5_STATE = 64
S5_DT_MIN = 1e-3
S5_DT_MAX = 1e-1

ML_HEADS = 4
ML_DV = 192
ML_DQK = ML_DV // 2
ML_WIDTH = ML_HEADS * ML_DV
ML_QK_DIM = ML_HEADS * ML_DQK
ML_CHUNK = 64

SSD_WIDTH = D_MIX - S5_WIDTH - ML_WIDTH
SSD_HEADDIM = 64
SSD_HEADS = SSD_WIDTH // SSD_HEADDIM
SSD_GROUPS = 4
SSD_RATIO = SSD_HEADS // SSD_GROUPS
SSD_STATE = 128
SSD_CHUNK = 128
SSD_CONV_DIM = SSD_WIDTH + 2 * SSD_GROUPS * SSD_STATE
SSD_DT_MIN = 1e-3
SSD_DT_MAX = 1e-1

IN_SIZES = (S5_WIDTH, S5_WIDTH,
            2 * ML_QK_DIM, ML_WIDTH,
            ML_HEADS, ML_HEADS, ML_WIDTH,
            ML_WIDTH,
            SSD_CONV_DIM, SSD_HEADS, SSD_WIDTH)
N_IN = sum(IN_SIZES)

DEEPNORM_ALPHA = (2.0 * DEPTH) ** 0.25
DEEPNORM_BETA = (8.0 * DEPTH) ** -0.25
EPS = 1e-5

kernel_name = 'hybrid_s5_mlstm_ssd_parallel_heads'


def split_in(h):
    idx = []
    acc = 0
    for s in IN_SIZES[:-1]:
        acc += s
        idx.append(acc)
    return jnp.split(h, idx, axis=-1)


def layer_norm(x, g, b):
    xf = x.astype(jnp.float32)
    mu = jnp.mean(xf, -1, keepdims=True)
    var = jnp.mean(jnp.square(xf - mu), -1, keepdims=True)
    y = (xf - mu) * lax.rsqrt(var + EPS) * g.astype(jnp.float32) + b.astype(jnp.float32)
    return y.astype(x.dtype)


def rms_norm(x, g):
    xf = x.astype(jnp.float32)
    return xf * lax.rsqrt(jnp.mean(jnp.square(xf), -1, keepdims=True) + EPS) * g.astype(jnp.float32)


def causal_dwconv(u, w, b):
    k = w.shape[0]
    out = lax.conv_general_dilated(u, w.astype(u.dtype)[:, None, :], window_strides=(1,),
                                   padding=[(k - 1, 0)], dimension_numbers=('NWC', 'WIO', 'NWC'),
                                   feature_group_count=u.shape[-1])
    return out + b.astype(u.dtype)


def s5_mixer(u, lam_re, lam_im, log_step, b_re, b_im, c_re, c_im, d, w_glu, b_glu):
    f32 = jnp.float32
    bsz, seq, _ = u.shape
    lam = lax.complex(lam_re.astype(f32), lam_im.astype(f32))
    step = jnp.exp(log_step.astype(f32))[:, None]
    lam_bar = jnp.exp(lam * step)
    b_mat = lax.complex(b_re.astype(f32), b_im.astype(f32))
    b_bar = ((lam_bar - 1.0) / lam)[..., None] * b_mat
    c_mat = lax.complex(c_re.astype(f32), c_im.astype(f32))
    uf = u.astype(f32)
    ug = uf.reshape(bsz, seq, S5_NGROUPS, S5_GROUP).astype(jnp.complex64)
    bu = jnp.einsum('blgc,gpc->blgp', ug, b_bar)
    a = jnp.broadcast_to(lam_bar, bu.shape)

    def combine(left, right):
        a_l, b_l = left
        a_r, b_r = right
        return a_r * a_l, a_r * b_l + b_r

    _, states = lax.associative_scan(combine, (a, bu), axis=1)
    y = jnp.einsum('blgp,gcp->blgc', states, c_mat).real.reshape(bsz, seq, S5_WIDTH)
    y = y + d.astype(f32) * uf
    g = jax.nn.gelu(y)
    return g * jax.nn.sigmoid(g @ w_glu.astype(f32) + b_glu.astype(f32))


def mlstm_mixer(q, k, v, i_pre, f_pre, o_pre, norm_g):
    f32 = jnp.float32
    bsz, seq, _ = q.shape
    nc = seq // ML_CHUNK

    def to_chunks(t, dim):
        return t.astype(f32).reshape(bsz, nc, ML_CHUNK, ML_HEADS, dim).transpose(1, 0, 3, 2, 4)

    def gate_chunks(t):
        return t.reshape(bsz, nc, ML_CHUNK, ML_HEADS).transpose(1, 0, 3, 2)

    qc = to_chunks(q, ML_DQK) * (ML_DQK ** -0.5)
    kc = to_chunks(k, ML_DQK)
    vc = to_chunks(v, ML_DV)
    ic = gate_chunks(i_pre.astype(f32))
    lfc = gate_chunks(jax.nn.log_sigmoid(f_pre.astype(f32)))
    causal = jnp.tril(jnp.ones((ML_CHUNK, ML_CHUNK), dtype=bool))

    def step(carry, inp):
        c_st, n_st, m_st = carry
        qj, kj, vj, ij, lf = inp
        bcum = jnp.cumsum(lf, axis=-1)
        log_d = jnp.where(causal, bcum[..., :, None] - bcum[..., None, :] + ij[..., None, :], -jnp.inf)
        inter = bcum + m_st[..., None]
        m_row = jnp.maximum(inter, jnp.max(log_d, -1))
        dmat = jnp.exp(log_d - m_row[..., None])
        inter_scale = jnp.exp(inter - m_row)
        s = jnp.einsum('bhid,bhjd->bhij', qj, kj) * dmat
        num = jnp.einsum('bhij,bhjv->bhiv', s, vj) + inter_scale[..., None] * jnp.einsum('bhid,bhdv->bhiv', qj, c_st)
        den = jnp.sum(s, -1) + inter_scale * jnp.einsum('bhid,bhd->bhi', qj, n_st)
        h = num / jnp.maximum(jnp.abs(den), jnp.exp(-m_row))[..., None]
        b_last = bcum[..., -1]
        log_w = b_last[..., None] - bcum + ij
        m_new = jnp.maximum(b_last + m_st, jnp.max(log_w, -1))
        w = jnp.exp(log_w - m_new[..., None])
        decay = jnp.exp(b_last + m_st - m_new)
        c_new = decay[..., None, None] * c_st + jnp.einsum('bhj,bhjd,bhjv->bhdv', w, kj, vj)
        n_new = decay[..., None] * n_st + jnp.einsum('bhj,bhjd->bhd', w, kj)
        return (c_new, n_new, m_new), h

    init = (jnp.zeros((bsz, ML_HEADS, ML_DQK, ML_DV), f32),
            jnp.zeros((bsz, ML_HEADS, ML_DQK), f32),
            jnp.zeros((bsz, ML_HEADS), f32))
    _, hs = lax.scan(step, init, (qc, kc, vc, ic, lfc))
    h = hs.transpose(1, 0, 3, 2, 4)
    h = h * lax.rsqrt(jnp.mean(jnp.square(h), -1, keepdims=True) + EPS)
    h = h.reshape(bsz, seq, ML_WIDTH) * norm_g.astype(f32)
    return jax.nn.sigmoid(o_pre.astype(f32)) * h


def ssd_mixer(xs, bs, cs, dt_raw, dt_bias, a_log, d_skip):
    f32 = jnp.float32
    bsz, seq, _ = xs.shape
    nc = seq // SSD_CHUNK
    shp = (bsz, nc, SSD_CHUNK, SSD_GROUPS, SSD_RATIO)
    x = xs.astype(f32).reshape(shp + (SSD_HEADDIM,))
    bm = bs.astype(f32).reshape(bsz, nc, SSD_CHUNK, SSD_GROUPS, SSD_STATE)
    cm = cs.astype(f32).reshape(bsz, nc, SSD_CHUNK, SSD_GROUPS, SSD_STATE)
    dt = jax.nn.softplus(dt_raw.astype(f32) + dt_bias.astype(f32)).reshape(shp)
    a = -jnp.exp(a_log.astype(f32)).reshape(SSD_GROUPS, SSD_RATIO)
    a_cum = jnp.cumsum(dt * a, axis=2)
    dtx = x * dt[..., None]
    causal = jnp.tril(jnp.ones((SSD_CHUNK, SSD_CHUNK), dtype=bool))
    seg = a_cum[:, :, :, None] - a_cum[:, :, None, :]
    lmat = jnp.exp(jnp.where(causal[:, :, None, None], seg, -jnp.inf))
    cb = jnp.einsum('bcign,bcjgn->bcijg', cm, bm)
    y_diag = jnp.einsum('bcijg,bcijgr,bcjgrp->bcigrp', cb, lmat, dtx)
    decay_s = jnp.exp(a_cum[:, :, -1:] - a_cum)
    states = jnp.einsum('bcjgn,bcjgr,bcjgrp->bcgrpn', bm, decay_s, dtx)
    chunk_decay = jnp.exp(a_cum[:, :, -1])

    def step(s_prev, inp):
        st, dec = inp
        return dec[..., None, None] * s_prev + st, s_prev

    init = jnp.zeros((bsz, SSD_GROUPS, SSD_RATIO, SSD_HEADDIM, SSD_STATE), f32)
    _, s_in = lax.scan(step, init, (states.transpose(1, 0, 2, 3, 4, 5), chunk_decay.transpose(1, 0, 2, 3)))
    s_in = s_in.transpose(1, 0, 2, 3, 4, 5)
    y_off = jnp.einsum('bcign,bcgrpn,bcigr->bcigrp', cm, s_in, jnp.exp(a_cum))
    y = y_diag + y_off + d_skip.astype(f32).reshape(SSD_GROUPS, SSD_RATIO)[..., None] * x
    return y.reshape(bsz, seq, SSD_WIDTH)


def setup_inputs(seed: int = 0) -> dict:
    key = jax.random.key(seed)
    ks = jax.random.split(key, 32)
    f32 = jnp.float32

    def nrm(k, shape, scale):
        return jax.random.normal(k, shape, f32) * scale

    x = jax.random.normal(ks[0], (BATCH, SEQ, D_MODEL), f32)
    w_in = nrm(ks[1], (DEPTH, D_MODEL, N_IN), D_MODEL ** -0.5)
    w_out = nrm(ks[2], (DEPTH, D_MIX, D_MODEL), DEEPNORM_BETA * D_MIX ** -0.5)
    ln_g = 1.0 + nrm(ks[3], (DEPTH, D_MODEL), 0.02)
    ln_b = nrm(ks[4], (DEPTH, D_MODEL), 0.02)
    n_idx = jnp.arange(S5_STATE, dtype=f32)
    s5_lambda_re = -0.5 + nrm(ks[5], (DEPTH, S5_NGROUPS, S5_STATE), 0.01)
    s5_lambda_im = math.pi * n_idx + nrm(ks[6], (DEPTH, S5_NGROUPS, S5_STATE), 0.01)
    s5_log_step = jax.random.uniform(ks[7], (DEPTH, S5_NGROUPS), f32, math.log(S5_DT_MIN), math.log(S5_DT_MAX))
    s5_b_re = nrm(ks[8], (DEPTH, S5_NGROUPS, S5_STATE, S5_GROUP), (2.0 * S5_GROUP) ** -0.5)
    s5_b_im = nrm(ks[9], (DEPTH, S5_NGROUPS, S5_STATE, S5_GROUP), (2.0 * S5_GROUP) ** -0.5)
    s5_c_re = nrm(ks[10], (DEPTH, S5_NGROUPS, S5_GROUP, S5_STATE), S5_STATE ** -0.5)
    s5_c_im = nrm(ks[11], (DEPTH, S5_NGROUPS, S5_GROUP, S5_STATE), S5_STATE ** -0.5)
    s5_d = nrm(ks[12], (DEPTH, S5_WIDTH), 1.0)
    s5_w_glu = nrm(ks[13], (DEPTH, S5_WIDTH, S5_WIDTH), S5_WIDTH ** -0.5)
    s5_b_glu = nrm(ks[14], (DEPTH, S5_WIDTH), 0.02)
    ml_conv_w = nrm(ks[15], (DEPTH, CONV_K, 2 * ML_QK_DIM), CONV_K ** -0.5)
    ml_conv_b = nrm(ks[16], (DEPTH, 2 * ML_QK_DIM), 0.02)
    ml_i_bias = nrm(ks[17], (DEPTH, ML_HEADS), 0.1)
    ml_f_bias = jnp.linspace(3.0, 6.0, ML_HEADS, dtype=f32) + nrm(ks[18], (DEPTH, ML_HEADS), 0.1)
    ml_norm_g = 1.0 + nrm(ks[19], (DEPTH, ML_WIDTH), 0.02)
    ssd_conv_w = nrm(ks[20], (DEPTH, CONV_K, SSD_CONV_DIM), CONV_K ** -0.5)
    ssd_conv_b = nrm(ks[21], (DEPTH, SSD_CONV_DIM), 0.02)
    dt0 = jnp.exp(jax.random.uniform(ks[22], (DEPTH, SSD_HEADS), f32, math.log(SSD_DT_MIN), math.log(SSD_DT_MAX)))
    ssd_dt_bias = dt0 + jnp.log(-jnp.expm1(-dt0))
    ssd_a_log = jnp.log(jax.random.uniform(ks[23], (DEPTH, SSD_HEADS), f32, 1.0, 16.0))
    ssd_d = 1.0 + nrm(ks[24], (DEPTH, SSD_HEADS), 0.1)
    ssd_norm_g = 1.0 + nrm(ks[25], (DEPTH, SSD_WIDTH), 0.02)
    return {'x': x, 'w_in': w_in, 'w_out': w_out, 'ln_g': ln_g, 'ln_b': ln_b,
            's5_lambda_re': s5_lambda_re, 's5_lambda_im': s5_lambda_im, 's5_log_step': s5_log_step,
            's5_b_re': s5_b_re, 's5_b_im': s5_b_im, 's5_c_re': s5_c_re, 's5_c_im': s5_c_im,
            's5_d': s5_d, 's5_w_glu': s5_w_glu, 's5_b_glu': s5_b_glu,
            'ml_conv_w': ml_conv_w, 'ml_conv_b': ml_conv_b, 'ml_i_bias': ml_i_bias,
            'ml_f_bias': ml_f_bias, 'ml_norm_g': ml_norm_g,
            'ssd_conv_w': ssd_conv_w, 'ssd_conv_b': ssd_conv_b, 'ssd_dt_bias': ssd_dt_bias,
            'ssd_a_log': ssd_a_log, 'ssd_d': ssd_d, 'ssd_norm_g': ssd_norm_g}


def reference(x, w_in, w_out, ln_g, ln_b,
              s5_lambda_re, s5_lambda_im, s5_log_step, s5_b_re, s5_b_im, s5_c_re, s5_c_im,
              s5_d, s5_w_glu, s5_b_glu,
              ml_conv_w, ml_conv_b, ml_i_bias, ml_f_bias, ml_norm_g,
              ssd_conv_w, ssd_conv_b, ssd_dt_bias, ssd_a_log, ssd_d, ssd_norm_g):
    f32 = jnp.float32
    for l in range(DEPTH):
        h = jnp.einsum('bld,dn->bln', x, w_in[l])
        (s5_u, s5_z, ml_qk, ml_v, ml_i, ml_f, ml_o, ml_z,
         ssd_xbc, ssd_dt, ssd_z) = split_in(h)
        y_s5 = s5_mixer(s5_u, s5_lambda_re[l], s5_lambda_im[l], s5_log_step[l],
                        s5_b_re[l], s5_b_im[l], s5_c_re[l], s5_c_im[l],
                        s5_d[l], s5_w_glu[l], s5_b_glu[l]) * jax.nn.silu(s5_z.astype(f32))
        qk = jax.nn.silu(causal_dwconv(ml_qk, ml_conv_w[l], ml_conv_b[l]))
        q, k = jnp.split(qk, 2, axis=-1)
        y_ml = mlstm_mixer(q, k, ml_v, ml_i.astype(f32) + ml_i_bias[l].astype(f32),
                           ml_f.astype(f32) + ml_f_bias[l].astype(f32), ml_o,
                           ml_norm_g[l]) * jax.nn.silu(ml_z.astype(f32))
        xbc = jax.nn.silu(causal_dwconv(ssd_xbc, ssd_conv_w[l], ssd_conv_b[l]))
        xs, bs, cs = jnp.split(xbc, [SSD_WIDTH, SSD_WIDTH + SSD_GROUPS * SSD_STATE], axis=-1)
        y = ssd_mixer(xs, bs, cs, ssd_dt, ssd_dt_bias[l], ssd_a_log[l], ssd_d[l])
        y_ssd = rms_norm(y * jax.nn.silu(ssd_z.astype(f32)), ssd_norm_g[l])
        mixed = jnp.concatenate([y_s5, y_ml, y_ssd], axis=-1).astype(x.dtype)
        out = jnp.einsum('bln,nd->bld', mixed, w_out[l])
        x = layer_norm(DEEPNORM_ALPHA * x + out, ln_g[l], ln_b[l])
    return x
```

```python
import functools
import math

import jax
import jax.numpy as jnp
from jax import lax
from jax.experimental import pallas as pl
from jax.experimental.pallas import tpu as pltpu

F32 = jnp.float32
BF16 = jnp.bfloat16
HI = lax.Precision.HIGHEST

LANES = 128
TAIL = 8
VMEM_LIMIT = 56 * 1024 * 1024

CONV_K = 4
EPS = 1e-5

S5_GROUP = 16
S5_STATE = 64
S5_CHUNK = 64

ML_HEADS = 4
ML_DV = 192
ML_DQK = 96
ML_DQK_PAD = 128
ML_CHUNK = 64
ML_WIN = 256

SSD_HEADDIM = 64
SSD_HEADS = 12
SSD_GROUPS = 4
SSD_RATIO = SSD_HEADS // SSD_GROUPS
SSD_STATE = 128
SSD_CHUNK = 128


def _silu(x):
    return x * jax.nn.sigmoid(x)


def _softplus(x):
    return jnp.maximum(x, 0.0) + jnp.log1p(jnp.exp(-jnp.abs(x)))


def _bdot(a, b):
    return jnp.dot(a.astype(BF16), b.astype(BF16), preferred_element_type=F32)


def _bdot_t0(a, b):
    return lax.dot_general(a.astype(BF16), b.astype(BF16), (((0,), (0,)), ((), ())),
                           preferred_element_type=F32)


def _bdot_t1(a, b):
    return lax.dot_general(a.astype(BF16), b.astype(BF16), (((1,), (1,)), ((), ())),
                           preferred_element_type=F32)


def _causal_conv(raw, buf_ref, cw_ref, cb_ref, first):
    q = raw.shape[0]

    @pl.when(first)
    def _():
        buf_ref[0:TAIL, :] = jnp.zeros((TAIL, raw.shape[1]), F32)

    buf_ref[TAIL:TAIL + q, :] = raw
    acc = cb_ref[...] + cw_ref[CONV_K - 1:CONV_K, :] * raw
    for j in range(CONV_K - 1):
        off = TAIL - (CONV_K - 1) + j
        acc = acc + cw_ref[j:j + 1, :] * buf_ref[off:off + q, :]
    buf_ref[0:TAIL, :] = raw[q - TAIL:q, :]
    return acc


def _inproj_kernel(x_ref, w_ref, o_ref, *, tn):
    xb = x_ref[...].astype(BF16)
    n = o_ref.shape[1]
    for j in range(0, n, tn):
        e = min(j + tn, n)
        o_ref[:, j:e] = jnp.dot(xb, w_ref[:, j:e], preferred_element_type=F32)


def _inproj(x2, w, tm):
    t, k = x2.shape
    n = w.shape[1]
    return pl.pallas_call(
        functools.partial(_inproj_kernel, tn=512),
        grid=(t // tm,),
        in_specs=[pl.BlockSpec((tm, k), lambda i: (i, 0)),
                  pl.BlockSpec((k, n), lambda i: (0, 0))],
        out_specs=pl.BlockSpec((tm, n), lambda i: (i, 0)),
        out_shape=jax.ShapeDtypeStruct((t, n), F32),
        compiler_params=pltpu.CompilerParams(dimension_semantics=("arbitrary",),
                                             vmem_limit_bytes=VMEM_LIMIT),
        name="inproj",
    )(x2, w)


def _outproj_kernel(x_ref, a_ref, b_ref, c_ref, wa_ref, wb_ref, wc_ref, g_ref, bb_ref, o_ref, *, alpha):
    out = (_bdot(a_ref[...], wa_ref[...]) + _bdot(b_ref[...], wb_ref[...])
           + _bdot(c_ref[...], wc_ref[...]))
    y = alpha * x_ref[...] + out
    mu = jnp.mean(y, -1, keepdims=True)
    d = y - mu
    var = jnp.mean(d * d, -1, keepdims=True)
    o_ref[...] = d * lax.rsqrt(var + EPS) * g_ref[...] + bb_ref[...]


def _outproj(x2, ma, mb, mc, wa, wb, wc, g, b, tm, alpha):
    t, d = x2.shape
    row = lambda i: (i, 0)
    fix = lambda i: (0, 0)
    return pl.pallas_call(
        functools.partial(_outproj_kernel, alpha=alpha),
        grid=(t // tm,),
        in_specs=[pl.BlockSpec((tm, d), row),
                  pl.BlockSpec((tm, ma.shape[1]), row),
                  pl.BlockSpec((tm, mb.shape[1]), row),
                  pl.BlockSpec((tm, mc.shape[1]), row),
                  pl.BlockSpec(wa.shape, fix), pl.BlockSpec(wb.shape, fix), pl.BlockSpec(wc.shape, fix),
                  pl.BlockSpec((1, d), fix), pl.BlockSpec((1, d), fix)],
        out_specs=pl.BlockSpec((tm, d), row),
        out_shape=jax.ShapeDtypeStruct((t, d), F32),
        compiler_params=pltpu.CompilerParams(dimension_semantics=("arbitrary",),
                                             vmem_limit_bytes=VMEM_LIMIT),
        name="outproj_ln",
    )(x2, ma, mb, mc, wa, wb, wc, g, b)


def _s5_scan_kernel(u_ref, tt_ref, wt_ref, vt_ref, rot_ref, y_ref, sloc_ref, sin_ref, *, bsz, nchunk):
    u = u_ref[0]
    sloc_ref[...] = jnp.dot(u, wt_ref[0], preferred_element_type=F32)
    a1 = rot_ref[0, 0:1, :]
    a2 = rot_ref[0, 1:2, :]

    def body(c, s):
        r = pl.ds(c * bsz, bsz)
        sin_ref[r, :] = s
        return a1 * s + a2 * pltpu.roll(s, S5_STATE, 1) + sloc_ref[r, :]

    lax.fori_loop(0, nchunk, body, jnp.zeros((bsz, 2 * S5_STATE), F32))
    y_ref[0] = (jnp.dot(u, tt_ref[0], preferred_element_type=F32)
                + _bdot(sin_ref[...], vt_ref[0]))


def _s5_scan(ut, tt, wt, vt, rot, bsz):
    g, n, qc = ut.shape
    grp = lambda i: (i, 0, 0)
    return pl.pallas_call(
        functools.partial(_s5_scan_kernel, bsz=bsz, nchunk=n // bsz),
        grid=(g,),
        in_specs=[pl.BlockSpec((1, n, qc), grp), pl.BlockSpec((1, qc, qc), grp),
                  pl.BlockSpec((1, qc, 2 * S5_STATE), grp), pl.BlockSpec((1, 2 * S5_STATE, qc), grp),
                  pl.BlockSpec((1, 2, 2 * S5_STATE), grp)],
        out_specs=pl.BlockSpec((1, n, qc), grp),
        out_shape=jax.ShapeDtypeStruct((g, n, qc), F32),
        scratch_shapes=[pltpu.VMEM((n, 2 * S5_STATE), F32), pltpu.VMEM((n, 2 * S5_STATE), F32)],
        compiler_params=pltpu.CompilerParams(dimension_semantics=("arbitrary",),
                                             vmem_limit_bytes=VMEM_LIMIT),
        name="s5_scan",
    )(ut, tt, wt, vt, rot)


def _s5_gate_kernel(y_ref, h_ref, d_ref, w_ref, b_ref, o_ref):
    w = y_ref.shape[1]
    u = h_ref[:, 0:w]
    z = h_ref[:, w:2 * w]
    y = y_ref[...] + d_ref[...] * u
    c = math.sqrt(2.0 / math.pi)
    g = 0.5 * y * (1.0 + jnp.tanh(c * (y + 0.044715 * (y * y * y))))
    gate = jax.nn.sigmoid(_bdot(g, w_ref[...]) + b_ref[...])
    o_ref[...] = g * gate * _silu(z)


def _s5_gate(y, h, d, w, b, tm):
    t, wd = y.shape
    row = lambda i: (i, 0)
    fix = lambda i: (0, 0)
    return pl.pallas_call(
        _s5_gate_kernel,
        grid=(t // tm,),
        in_specs=[pl.BlockSpec((tm, wd), row), pl.BlockSpec((tm, 2 * wd), row),
                  pl.BlockSpec((1, wd), fix), pl.BlockSpec((wd, wd), fix), pl.BlockSpec((1, wd), fix)],
        out_specs=pl.BlockSpec((tm, wd), row),
        out_shape=jax.ShapeDtypeStruct((t, wd), F32),
        compiler_params=pltpu.CompilerParams(dimension_semantics=("arbitrary",),
                                             vmem_limit_bytes=VMEM_LIMIT),
        name="s5_gate",
    )(y, h, d, w, b)


def _s5_tables(lam_re, lam_im, log_step, b_re, b_im, c_re, c_im):
    q = S5_CHUNK
    step = jnp.exp(log_step)[..., None]
    zr, zi = lam_re * step, lam_im * step
    er = jnp.exp(zr)
    lbr, lbi = er * jnp.cos(zi), er * jnp.sin(zi)
    nr, ni = lbr - 1.0, lbi
    den = lam_re * lam_re + lam_im * lam_im
    fr = (nr * lam_re + ni * lam_im) / den
    fi = (ni * lam_re - nr * lam_im) / den
    bbr = fr[..., None] * b_re - fi[..., None] * b_im
    bbi = fr[..., None] * b_im + fi[..., None] * b_re
    dd = jnp.arange(q + 1, dtype=F32)[:, None, None, None]
    pm = jnp.exp(dd * zr[None])
    pr, pi = pm * jnp.cos(dd * zi[None]), pm * jnp.sin(dd * zi[None])
    cpr = c_re[None] * pr[:, :, :, None, :] - c_im[None] * pi[:, :, :, None, :]
    cpi = c_re[None] * pi[:, :, :, None, :] + c_im[None] * pr[:, :, :, None, :]
    kk = (jnp.einsum('qdgop,dgpi->dgqoi', cpr[:q], bbr, precision=HI)
          - jnp.einsum('qdgop,dgpi->dgqoi', cpi[:q], bbi, precision=HI))
    s_idx = jnp.arange(q)[:, None]
    t_idx = jnp.arange(q)[None, :]
    lag = jnp.clip(t_idx - s_idx, 0, q - 1)
    tt = jnp.where((t_idx >= s_idx)[None, None, :, :, None, None], kk[:, :, lag], 0.0)
    tt = tt.transpose(0, 1, 2, 5, 3, 4).reshape(tt.shape[0], tt.shape[1], q * S5_GROUP, q * S5_GROUP)
    rr, ri = pr[:q][::-1], pi[:q][::-1]
    wr = rr[..., None] * bbr[None] - ri[..., None] * bbi[None]
    wi = rr[..., None] * bbi[None] + ri[..., None] * bbr[None]
    wt = jnp.concatenate([wr, wi], axis=3)
    wt = wt.transpose(1, 2, 0, 4, 3).reshape(wt.shape[1], wt.shape[2], q * S5_GROUP, 2 * S5_STATE)
    vt = jnp.concatenate([cpr[1:], -cpi[1:]], axis=4)
    vt = vt.transpose(1, 2, 4, 0, 3).reshape(vt.shape[1], vt.shape[2], 2 * S5_STATE, q * S5_GROUP)
    rot = jnp.stack([jnp.concatenate([pr[q], pr[q]], -1),
                     jnp.concatenate([-pi[q], pi[q]], -1)], axis=2)
    return tt.astype(BF16), wt.astype(BF16), vt.astype(BF16), rot


def _s5_mixer(h, tabs, d, w_glu, b_glu, bsz, tm):
    tt, wt, vt, rot = tabs
    t = h.shape[0]
    wd = h.shape[1] // 2
    g = wd // S5_GROUP
    nc = t // (bsz * S5_CHUNK)
    u = h[:, :wd].astype(BF16).reshape(bsz, nc, S5_CHUNK, g, S5_GROUP)
    ut = u.transpose(3, 1, 0, 2, 4).reshape(g, nc * bsz, S5_CHUNK * S5_GROUP)
    yt = _s5_scan(ut, tt, wt, vt, rot, bsz)
    y = yt.reshape(g, nc, bsz, S5_CHUNK, S5_GROUP).transpose(2, 1, 3, 0, 4).reshape(t, wd)
    return _s5_gate(y, h, d, w_glu, b_glu, tm)


def _mlstm_kernel(h_ref, cw_ref, cb_ref, gb_ref, ng_ref, o_ref, c_ref, m_ref, buf_ref):
    q = ML_CHUNK
    nqk = 2 * ML_HEADS * ML_DQK_PAD
    wv = ML_HEADS * ML_DV
    first = pl.program_id(1) == 0

    @pl.when(first)
    def _():
        c_ref[...] = jnp.zeros(c_ref.shape, F32)
        m_ref[...] = jnp.zeros(m_ref.shape, F32)

    qk = _silu(_causal_conv(h_ref[:, 0:nqk], buf_ref, cw_ref, cb_ref, first))
    v = h_ref[:, nqk:nqk + wv]
    o_pre = h_ref[:, nqk + wv:nqk + 2 * wv]
    z = h_ref[:, nqk + 2 * wv:nqk + 3 * wv]
    gates = h_ref[:, nqk + 3 * wv:nqk + 3 * wv + LANES] + gb_ref[...]
    lsg = -_softplus(-gates)

    row = lax.broadcasted_iota(jnp.int32, (q, q), 0)
    col = lax.broadcasted_iota(jnp.int32, (q, q), 1)
    causal = col <= row
    bc = jnp.dot(causal.astype(F32), lsg, precision=HI, preferred_element_type=F32)
    gates_t = gates.T
    bc_t = bc.T
    lane = lax.broadcasted_iota(jnp.int32, (1, ML_WIN), 1)

    outs = []
    for hd in range(ML_HEADS):
        odd = hd % 2
        w0 = (hd // 2) * 2 * ML_DV + (LANES if odd else 0)
        vmask = (lane >= ML_WIN - ML_DV) if odd else (lane < ML_DV)
        nlane = 0 if odd else ML_DV
        vext = jnp.where(lane == nlane, 1.0, jnp.where(vmask, v[:, w0:w0 + ML_WIN], 0.0)).astype(BF16)
        qh = (qk[:, hd * ML_DQK_PAD:(hd + 1) * ML_DQK_PAD] * (ML_DQK ** -0.5)).astype(BF16)
        kh = qk[:, (ML_HEADS + hd) * ML_DQK_PAD:(ML_HEADS + hd + 1) * ML_DQK_PAD]
        bcol = bc[:, ML_HEADS + hd:ML_HEADS + hd + 1]
        brow = bc_t[ML_HEADS + hd:ML_HEADS + hd + 1, :]
        icol = gates[:, hd:hd + 1]
        irow = gates_t[hd:hd + 1, :]
        m_st = m_ref[hd, 0:1, 0:1]
        log_d = jnp.where(causal, bcol - brow + irow, -jnp.inf)
        inter = bcol + m_st
        m_row = jnp.maximum(inter, jnp.max(log_d, -1, keepdims=True))
        dmat = jnp.exp(log_d - m_row)
        isc = jnp.exp(inter - m_row)
        s = _bdot_t1(qh, kh) * dmat
        c_st = c_ref[hd]
        num = _bdot(s, vext) + isc * _bdot(qh, c_st)
        den = num[:, nlane:nlane + 1]
        hh = num / jnp.maximum(jnp.abs(den), jnp.exp(-m_row))
        hv = jnp.where(vmask, hh, 0.0)
        ms = jnp.sum(hv * hv, -1, keepdims=True) * (1.0 / ML_DV)
        outs.append(hv * lax.rsqrt(ms + EPS))
        blast = bc[q - 1:q, ML_HEADS + hd:ML_HEADS + hd + 1]
        log_w = blast - bcol + icol
        m_new = jnp.maximum(blast + m_st, jnp.max(log_w, 0, keepdims=True))
        wgt = jnp.exp(log_w - m_new)
        decay = jnp.exp(blast + m_st - m_new)
        c_ref[hd] = decay * c_st + _bdot_t0(kh * wgt, vext)
        m_ref[hd] = jnp.broadcast_to(m_new, m_ref.shape[1:])

    hl = LANES
    tiles = []
    for pr in range(ML_HEADS // 2):
        a, b = outs[2 * pr], outs[2 * pr + 1]
        tiles += [a[:, :hl], a[:, hl:] + b[:, :hl], b[:, hl:]]
    hn = jnp.concatenate(tiles, axis=1)
    o_ref[...] = jax.nn.sigmoid(o_pre) * (hn * ng_ref[...]) * _silu(z)


def _mlstm_mixer(h, cw, cb, gb, ng, bsz):
    t, n = h.shape
    nc = t // (bsz * ML_CHUNK)
    nqk = 2 * ML_HEADS * ML_DQK_PAD
    wv = ML_HEADS * ML_DV
    fix = lambda b, c: (0, 0)
    blk = lambda b, c: (b * nc + c, 0)
    return pl.pallas_call(
        _mlstm_kernel,
        grid=(bsz, nc),
        in_specs=[pl.BlockSpec((ML_CHUNK, n), blk),
                  pl.BlockSpec((CONV_K, nqk), fix), pl.BlockSpec((1, nqk), fix),
                  pl.BlockSpec((1, LANES), fix), pl.BlockSpec((1, wv), fix)],
        out_specs=pl.BlockSpec((ML_CHUNK, wv), blk),
        out_shape=jax.ShapeDtypeStruct((t, wv), F32),
        scratch_shapes=[pltpu.VMEM((ML_HEADS, ML_DQK_PAD, ML_WIN), F32),
                        pltpu.VMEM((ML_HEADS, TAIL, LANES), F32),
                        pltpu.VMEM((TAIL + ML_CHUNK, nqk), F32)],
        compiler_params=pltpu.CompilerParams(dimension_semantics=("arbitrary", "arbitrary"),
                                             vmem_limit_bytes=VMEM_LIMIT),
        name="mlstm",
    )(h, cw, cb, gb, ng)


def _ssd_kernel(h_ref, cw_ref, cb_ref, dtb_ref, alog_ref, dsk_ref, ng_ref, o_ref, s_ref, buf_ref):
    q = SSD_CHUNK
    wx = SSD_HEADS * SSD_HEADDIM
    wbc = SSD_GROUPS * SSD_STATE
    nconv = wx + 2 * wbc
    first = pl.program_id(1) == 0

    @pl.when(first)
    def _():
        s_ref[...] = jnp.zeros(s_ref.shape, F32)

    xbc = _silu(_causal_conv(h_ref[:, 0:nconv], buf_ref, cw_ref, cb_ref, first))
    z = h_ref[:, nconv:nconv + wx]
    dt = _softplus(h_ref[:, nconv + wx:nconv + wx + LANES] + dtb_ref[...])
    a = -jnp.exp(alog_ref[...])
    row = lax.broadcasted_iota(jnp.int32, (q, q), 0)
    col = lax.broadcasted_iota(jnp.int32, (q, q), 1)
    causal = col <= row
    acum = jnp.dot(causal.astype(F32), dt * a, precision=HI, preferred_element_type=F32)
    acum_t = acum.T
    alast = acum[q - 1:q, :]
    dec_s = jnp.exp(alast - acum)
    eac = jnp.exp(acum)
    cdec = jnp.exp(alast)
    lane = lax.broadcasted_iota(jnp.int32, (1, LANES), 1)
    lo = lane < SSD_HEADDIM

    bm = [xbc[:, wx + g * SSD_STATE:wx + (g + 1) * SSD_STATE] for g in range(SSD_GROUPS)]
    cm = [xbc[:, wx + wbc + g * SSD_STATE:wx + wbc + (g + 1) * SSD_STATE] for g in range(SSD_GROUPS)]
    cbm = [_bdot_t1(cm[g], bm[g]) for g in range(SSD_GROUPS)]

    for pr in range(SSD_HEADS // 2):
        xp = xbc[:, pr * LANES:(pr + 1) * LANES]
        h0, h1 = 2 * pr, 2 * pr + 1
        dtx = xp * jnp.where(lo, dt[:, h0:h0 + 1], dt[:, h1:h1 + 1])
        s_prev = s_ref[pr]
        y = dsk_ref[:, pr * LANES:(pr + 1) * LANES] * xp
        st = jnp.zeros((SSD_STATE, LANES), F32)
        for hd, msk in ((h0, lo), (h1, jnp.logical_not(lo))):
            g = hd // SSD_RATIO
            xm = jnp.where(msk, dtx, 0.0).astype(BF16)
            seg = acum[:, hd:hd + 1] - acum_t[hd:hd + 1, :]
            lmat = jnp.exp(jnp.where(causal, seg, -jnp.inf))
            y = y + _bdot(cbm[g] * lmat, xm)
            st = st + _bdot_t0(bm[g] * dec_s[:, hd:hd + 1], xm)
            y = y + _bdot(cm[g] * eac[:, hd:hd + 1], jnp.where(msk, s_prev, 0.0))
        s_ref[pr] = s_prev * jnp.where(lo, cdec[:, h0:h0 + 1], cdec[:, h1:h1 + 1]) + st
        o_ref[:, pr * LANES:(pr + 1) * LANES] = y

    yz = o_ref[...] * _silu(z)
    ms = jnp.mean(yz * yz, -1, keepdims=True)
    o_ref[...] = yz * lax.rsqrt(ms + EPS) * ng_ref[...]


def _ssd_mixer(h, cw, cb, dtb, alog, dsk, ng, bsz):
    t, n = h.shape
    nc = t // (bsz * SSD_CHUNK)
    wx = SSD_HEADS * SSD_HEADDIM
    nconv = wx + 2 * SSD_GROUPS * SSD_STATE
    fix = lambda b, c: (0, 0)
    blk = lambda b, c: (b * nc + c, 0)
    return pl.pallas_call(
        _ssd_kernel,
        grid=(bsz, nc),
        in_specs=[pl.BlockSpec((SSD_CHUNK, n), blk),
                  pl.BlockSpec((CONV_K, nconv), fix), pl.BlockSpec((1, nconv), fix),
                  pl.BlockSpec((1, LANES), fix), pl.BlockSpec((1, LANES), fix),
                  pl.BlockSpec((1, wx), fix), pl.BlockSpec((1, wx), fix)],
        out_specs=pl.BlockSpec((SSD_CHUNK, wx), blk),
        out_shape=jax.ShapeDtypeStruct((t, wx), F32),
        scratch_shapes=[pltpu.VMEM((SSD_HEADS // 2, SSD_STATE, LANES), F32),
                        pltpu.VMEM((TAIL + SSD_CHUNK, nconv), F32)],
        compiler_params=pltpu.CompilerParams(dimension_semantics=("arbitrary", "arbitrary"),
                                             vmem_limit_bytes=VMEM_LIMIT),
        name="ssd",
    )(h, cw, cb, dtb, alog, dsk, ng)


def _pad_last(a, n):
    return jnp.pad(a, [(0, 0)] * (a.ndim - 1) + [(0, n - a.shape[-1])])


def _pad_heads(a, heads, width, padded):
    s = a.shape[:-1]
    a = a.reshape(s + (heads, width))
    return _pad_last(a, padded).reshape(s + (heads * padded,))


def kernel(x, w_in, w_out, ln_g, ln_b, s5_lambda_re, s5_lambda_im, s5_log_step, s5_b_re, s5_b_im, s5_c_re, s5_c_im, s5_d, s5_w_glu, s5_b_glu, ml_conv_w, ml_conv_b, ml_i_bias, ml_f_bias, ml_norm_g, ssd_conv_w, ssd_conv_b, ssd_dt_bias, ssd_a_log, ssd_d, ssd_norm_g):
    bsz, seq, dm = x.shape
    depth = w_in.shape[0]
    t = bsz * seq
    s5w = s5_d.shape[-1]
    mlw = ML_HEADS * ML_DV
    mlqk = ML_HEADS * ML_DQK
    ssdw = SSD_HEADS * SSD_HEADDIM
    ssdbc = SSD_GROUPS * SSD_STATE
    alpha = (2.0 * depth) ** 0.25
    tm = min(256, seq)

    sizes = (s5w, s5w, 2 * mlqk, mlw, ML_HEADS, ML_HEADS, mlw, mlw, ssdw + 2 * ssdbc, SSD_HEADS, ssdw)
    offs = [0]
    for sz in sizes:
        offs.append(offs[-1] + sz)
    col = lambda i: w_in[:, :, offs[i]:offs[i + 1]]
    w_s5 = jnp.concatenate([col(0), col(1)], -1).astype(BF16)
    qkw = col(2)
    w_ml = jnp.concatenate([
        _pad_heads(qkw[..., :mlqk], ML_HEADS, ML_DQK, ML_DQK_PAD),
        _pad_heads(qkw[..., mlqk:], ML_HEADS, ML_DQK, ML_DQK_PAD),
        col(3), col(6), col(7),
        _pad_last(jnp.concatenate([col(4), col(5)], -1), LANES)], -1).astype(BF16)
    w_ssd = jnp.concatenate([col(8), col(10), _pad_last(col(9), LANES)], -1).astype(BF16)

    ml_cw = jnp.concatenate([_pad_heads(ml_conv_w[..., :mlqk], ML_HEADS, ML_DQK, ML_DQK_PAD),
                             _pad_heads(ml_conv_w[..., mlqk:], ML_HEADS, ML_DQK, ML_DQK_PAD)], -1)
    ml_cb = jnp.concatenate([_pad_heads(ml_conv_b[..., :mlqk], ML_HEADS, ML_DQK, ML_DQK_PAD),
                             _pad_heads(ml_conv_b[..., mlqk:], ML_HEADS, ML_DQK, ML_DQK_PAD)], -1)[:, None, :]
    ml_gb = _pad_last(jnp.concatenate([ml_i_bias, ml_f_bias], -1), LANES)[:, None, :]
    ssd_dtb = _pad_last(ssd_dt_bias, LANES)[:, None, :]
    ssd_al = _pad_last(ssd_a_log, LANES)[:, None, :]
    ssd_dsk = jnp.repeat(ssd_d, SSD_HEADDIM, axis=-1)[:, None, :]
    w_o = w_out.astype(BF16)

    tabs = _s5_tables(s5_lambda_re, s5_lambda_im, s5_log_step, s5_b_re, s5_b_im, s5_c_re, s5_c_im)

    layers = dict(
        w_s5=w_s5, w_ml=w_ml, w_ssd=w_ssd,
        wo_a=w_o[:, :s5w], wo_b=w_o[:, s5w:s5w + mlw], wo_c=w_o[:, s5w + mlw:],
        ln_g=ln_g[:, None, :], ln_b=ln_b[:, None, :],
        tabs=tabs, s5_d=s5_d[:, None, :], s5_wg=s5_w_glu.astype(BF16), s5_bg=s5_b_glu[:, None, :],
        ml_cw=ml_cw, ml_cb=ml_cb, ml_gb=ml_gb, ml_ng=ml_norm_g[:, None, :],
        ssd_cw=ssd_conv_w, ssd_cb=ssd_conv_b[:, None, :], ssd_dtb=ssd_dtb, ssd_al=ssd_al,
        ssd_dsk=ssd_dsk, ssd_ng=ssd_norm_g[:, None, :])

    def layer(x2, p):
        h_s5 = _inproj(x2, p['w_s5'], tm)
        h_ml = _inproj(x2, p['w_ml'], tm)
        h_ssd = _inproj(x2, p['w_ssd'], tm)
        y_s5 = _s5_mixer(h_s5, p['tabs'], p['s5_d'], p['s5_wg'], p['s5_bg'], bsz, tm)
        y_ml = _mlstm_mixer(h_ml, p['ml_cw'], p['ml_cb'], p['ml_gb'], p['ml_ng'], bsz)
        y_ssd = _ssd_mixer(h_ssd, p['ssd_cw'], p['ssd_cb'], p['ssd_dtb'], p['ssd_al'],
                           p['ssd_dsk'], p['ssd_ng'], bsz)
        x2 = _outproj(x2, y_s5, y_ml, y_ssd, p['wo_a'], p['wo_b'], p['wo_c'],
                      p['ln_g'], p['ln_b'], tm, alpha)
        return x2, None

    x2, _ = lax.scan(layer, x.reshape(t, dm), layers)
    return x2.reshape(bsz, seq, dm)
```

```python
import functools
import math

import jax
import jax.numpy as jnp
from jax import lax
from jax.experimental import pallas as pl
from jax.experimental.pallas import tpu as pltpu

F32 = jnp.float32
BF16 = jnp.bfloat16
HI = lax.Precision.HIGHEST

LANES = 128
TAIL = 8
VMEM_LIMIT = 56 * 1024 * 1024

CONV_K = 4
EPS = 1e-5

S5_GROUP = 16
S5_STATE = 64
S5_CHUNK = LANES

ML_HEADS = 4
ML_DV = 192
ML_DQK = 96
ML_DQK_PAD = 128
ML_CHUNK = 128
ML_WIN = 256

SSD_HEADDIM = 64
SSD_HEADS = 12
SSD_GROUPS = 4
SSD_RATIO = SSD_HEADS // SSD_GROUPS
SSD_STATE = 128
SSD_CHUNK = 128


def _silu(x):
    return x * jax.nn.sigmoid(x)


def _softplus(x):
    return jnp.maximum(x, 0.0) + jnp.log1p(jnp.exp(-jnp.abs(x)))


def _bdot(a, b):
    return jnp.dot(a.astype(BF16), b.astype(BF16), preferred_element_type=F32)


def _bdot_t0(a, b):
    return lax.dot_general(a.astype(BF16), b.astype(BF16), (((0,), (0,)), ((), ())),
                           preferred_element_type=F32)


def _bdot_t1(a, b):
    return lax.dot_general(a.astype(BF16), b.astype(BF16), (((1,), (1,)), ((), ())),
                           preferred_element_type=F32)


def _causal_conv(raw, buf_ref, cw_ref, cb_ref, first):
    q = raw.shape[0]

    @pl.when(first)
    def _():
        buf_ref[0:TAIL, :] = jnp.zeros((TAIL, raw.shape[1]), F32)

    buf_ref[TAIL:TAIL + q, :] = raw
    acc = cb_ref[...] + cw_ref[CONV_K - 1:CONV_K, :] * raw
    for j in range(CONV_K - 1):
        off = TAIL - (CONV_K - 1) + j
        acc = acc + cw_ref[j:j + 1, :] * buf_ref[off:off + q, :]
    buf_ref[0:TAIL, :] = raw[q - TAIL:q, :]
    return acc


def _resident(shape):
    zeros = (0,) * len(shape)
    return pl.BlockSpec(shape, lambda *_: zeros, pipeline_mode=pl.Buffered(1))


def _inproj_kernel(x_ref, ws5_ref, wml_ref, wssd_ref, os5_ref, oml_ref, ossd_ref, *, tn):
    xb = x_ref[...].astype(BF16)
    for j in range(0, os5_ref.shape[0], tn):
        os5_ref[j:j + tn, :] = lax.dot_general(ws5_ref[j:j + tn, :], xb, (((1,), (1,)), ((), ())),
                                               preferred_element_type=F32)
    for w_ref, o_ref in ((wml_ref, oml_ref), (wssd_ref, ossd_ref)):
        n = o_ref.shape[1]
        for j in range(0, n, tn):
            e = min(j + tn, n)
            o_ref[:, j:e] = jnp.dot(xb, w_ref[:, j:e], preferred_element_type=F32)


def _inproj(x2, ws5t, wml, wssd, tm):
    t, k = x2.shape
    ns5, nml, nssd = ws5t.shape[0], wml.shape[1], wssd.shape[1]
    return pl.pallas_call(
        functools.partial(_inproj_kernel, tn=512),
        grid=(t // tm,),
        in_specs=[pl.BlockSpec((tm, k), lambda i: (i, 0)),
                  _resident(ws5t.shape), _resident(wml.shape), _resident(wssd.shape)],
        out_specs=[pl.BlockSpec((ns5, tm), lambda i: (0, i)),
                   pl.BlockSpec((tm, nml), lambda i: (i, 0)),
                   pl.BlockSpec((tm, nssd), lambda i: (i, 0))],
        out_shape=[jax.ShapeDtypeStruct((ns5, t), F32),
                   jax.ShapeDtypeStruct((t, nml), F32),
                   jax.ShapeDtypeStruct((t, nssd), F32)],
        compiler_params=pltpu.CompilerParams(dimension_semantics=("arbitrary",),
                                             vmem_limit_bytes=VMEM_LIMIT),
        name="inproj",
    )(x2, ws5t, wml, wssd)


def _outproj_kernel(x_ref, a_ref, b_ref, c_ref, wa_ref, wb_ref, wc_ref, g_ref, bb_ref, o_ref, *, alpha):
    out = (_bdot_t0(a_ref[...], wa_ref[...]) + _bdot(b_ref[...], wb_ref[...])
           + _bdot(c_ref[...], wc_ref[...]))
    y = alpha * x_ref[...] + out
    mu = jnp.mean(y, -1, keepdims=True)
    d = y - mu
    var = jnp.mean(d * d, -1, keepdims=True)
    o_ref[...] = d * lax.rsqrt(var + EPS) * g_ref[...] + bb_ref[...]


def _outproj(x2, mat, mb, mc, wa, wb, wc, g, b, tm, alpha):
    t, d = x2.shape
    row = lambda i: (i, 0)
    return pl.pallas_call(
        functools.partial(_outproj_kernel, alpha=alpha),
        grid=(t // tm,),
        in_specs=[pl.BlockSpec((tm, d), row),
                  pl.BlockSpec((mat.shape[0], tm), lambda i: (0, i)),
                  pl.BlockSpec((tm, mb.shape[1]), row),
                  pl.BlockSpec((tm, mc.shape[1]), row),
                  _resident(wa.shape), _resident(wb.shape), _resident(wc.shape),
                  _resident((1, d)), _resident((1, d))],
        out_specs=pl.BlockSpec((tm, d), row),
        out_shape=jax.ShapeDtypeStruct((t, d), F32),
        compiler_params=pltpu.CompilerParams(dimension_semantics=("arbitrary",),
                                             vmem_limit_bytes=VMEM_LIMIT),
        name="outproj_ln",
    )(x2, mat, mb, mc, wa, wb, wc, g, b)


def _s5_scan_kernel(u_ref, kk_ref, wt_ref, vt_ref, rot_ref, y_ref, tt_ref, sloc_ref, sin_ref, *, bsz, nchunk):
    q = S5_CHUNK
    p2 = 2 * S5_STATE
    srow = lax.broadcasted_iota(jnp.int32, (q, q), 0)
    tcol = lax.broadcasted_iota(jnp.int32, (q, q), 1)
    upper = tcol >= srow

    def build(ci, carry):
        for co in range(S5_GROUP):
            lag = kk_ref[0, pl.ds(ci * S5_GROUP + co, 1), :]
            blk = pltpu.roll(jnp.broadcast_to(lag, (q, q)), 0, 1, stride=1, stride_axis=0)
            tt_ref[pl.ds(pl.multiple_of(ci * q, q), q), co * q:(co + 1) * q] = (
                jnp.where(upper, blk, 0.0).astype(BF16))
        return carry

    lax.fori_loop(0, S5_GROUP, build, 0)

    u = jnp.concatenate([u_ref[ci] for ci in range(S5_GROUP)], axis=1).astype(BF16)
    sloc_ref[...] = jnp.dot(u, wt_ref[0], preferred_element_type=F32)
    a1 = rot_ref[0, 0:1, :]
    a2 = rot_ref[0, 1:2, :]

    def body(c, states):
        nxt = []
        for b in range(bsz):
            r = pl.ds(b * nchunk + c, 1)
            s = states[b]
            sin_ref[r, :] = s
            swapped = jnp.concatenate([s[:, p2:], s[:, :p2]], axis=1)
            nxt.append(a1 * s + a2 * swapped + sloc_ref[r, :])
        return tuple(nxt)

    lax.fori_loop(0, nchunk, body, tuple(jnp.zeros((1, 2 * p2), F32) for _ in range(bsz)))
    y = (jnp.dot(u, tt_ref[...], preferred_element_type=F32)
         + _bdot(sin_ref[:, 0:p2], vt_ref[0]))
    for co in range(S5_GROUP):
        y_ref[co] = y[:, co * q:(co + 1) * q]


def _s5_scan(u3, kk, wt, vt, rot, bsz):
    w, n, q = u3.shape
    g = w // S5_GROUP
    qc = S5_GROUP * q
    p2 = 2 * S5_STATE
    grp = lambda i: (i, 0, 0)
    return pl.pallas_call(
        functools.partial(_s5_scan_kernel, bsz=bsz, nchunk=n // bsz),
        grid=(g,),
        in_specs=[pl.BlockSpec((S5_GROUP, n, q), grp),
                  pl.BlockSpec((1, S5_GROUP * S5_GROUP, q), grp),
                  pl.BlockSpec((1, qc, 2 * p2), grp), pl.BlockSpec((1, p2, qc), grp),
                  pl.BlockSpec((1, 2, 2 * p2), grp)],
        out_specs=pl.BlockSpec((S5_GROUP, n, q), grp),
        out_shape=jax.ShapeDtypeStruct((w, n, q), F32),
        scratch_shapes=[pltpu.VMEM((qc, qc), BF16),
                        pltpu.VMEM((n, 2 * p2), F32), pltpu.VMEM((n, 2 * p2), F32)],
        compiler_params=pltpu.CompilerParams(dimension_semantics=("arbitrary",),
                                             vmem_limit_bytes=VMEM_LIMIT),
        name="s5_scan",
    )(u3, kk, wt, vt, rot)


def _s5_gate_kernel(y_ref, u_ref, z_ref, d_ref, w_ref, b_ref, o_ref):
    y = y_ref[...] + d_ref[...] * u_ref[...]
    c = math.sqrt(2.0 / math.pi)
    g = 0.5 * y * (1.0 + jnp.tanh(c * (y + 0.044715 * (y * y * y))))
    gate = jax.nn.sigmoid(_bdot(w_ref[...], g) + b_ref[...])
    o_ref[...] = g * gate * _silu(z_ref[...])


def _s5_gate(yt, ht, d, wgt, b, tm):
    wd, t = yt.shape
    return pl.pallas_call(
        _s5_gate_kernel,
        grid=(t // tm,),
        in_specs=[pl.BlockSpec((wd, tm), lambda i: (0, i)),
                  pl.BlockSpec((wd, tm), lambda i: (0, i)),
                  pl.BlockSpec((wd, tm), lambda i: (1, i)),
                  _resident((wd, 1)), _resident((wd, wd)), _resident((wd, 1))],
        out_specs=pl.BlockSpec((wd, tm), lambda i: (0, i)),
        out_shape=jax.ShapeDtypeStruct((wd, t), F32),
        compiler_params=pltpu.CompilerParams(dimension_semantics=("arbitrary",),
                                             vmem_limit_bytes=VMEM_LIMIT),
        name="s5_gate",
    )(yt, ht, ht, d, wgt, b)


def _s5_tables(lam_re, lam_im, log_step, b_re, b_im, c_re, c_im):
    q = S5_CHUNK
    dl, g = lam_re.shape[0], lam_re.shape[1]
    step = jnp.exp(log_step)[..., None]
    zr, zi = lam_re * step, lam_im * step
    er = jnp.exp(zr)
    lbr, lbi = er * jnp.cos(zi), er * jnp.sin(zi)
    nr, ni = lbr - 1.0, lbi
    den = lam_re * lam_re + lam_im * lam_im
    fr = (nr * lam_re + ni * lam_im) / den
    fi = (ni * lam_re - nr * lam_im) / den
    bbr = fr[..., None] * b_re - fi[..., None] * b_im
    bbi = fr[..., None] * b_im + fi[..., None] * b_re
    dd = jnp.arange(q + 1, dtype=F32)[:, None, None, None]
    pm = jnp.exp(dd * zr[None])
    pr, pi = pm * jnp.cos(dd * zi[None]), pm * jnp.sin(dd * zi[None])
    cpr = c_re[None] * pr[:, :, :, None, :] - c_im[None] * pi[:, :, :, None, :]
    cpi = c_re[None] * pi[:, :, :, None, :] + c_im[None] * pr[:, :, :, None, :]
    cpc = jnp.concatenate([cpr, -cpi], axis=4)
    kk = jnp.einsum('qdgop,dgpi->dgioq', cpc[:q], jnp.concatenate([bbr, bbi], axis=2), precision=HI)
    kk = kk.reshape(dl, g, S5_GROUP * S5_GROUP, q)
    rr, ri = pr[:q][::-1], pi[:q][::-1]
    wr = rr[..., None] * bbr[None] - ri[..., None] * bbi[None]
    wi = rr[..., None] * bbi[None] + ri[..., None] * bbr[None]
    wt = jnp.concatenate([wr, wi, wi, wr], axis=3)
    wt = wt.transpose(1, 2, 4, 0, 3).reshape(dl, g, S5_GROUP * q, 4 * S5_STATE)
    vt = cpc[1:].transpose(1, 2, 4, 3, 0).reshape(dl, g, 2 * S5_STATE, S5_GROUP * q)
    rot = jnp.stack([jnp.concatenate([pr[q]] * 4, -1),
                     jnp.concatenate([-pi[q], pi[q], pi[q], -pi[q]], -1)], axis=2)
    return kk, wt.astype(BF16), vt.astype(BF16), rot


def _s5_mixer(ht, tabs, d, wgt, bg, bsz, tm):
    kk, wt, vt, rot = tabs
    wd = ht.shape[0] // 2
    t = ht.shape[1]
    u3 = ht[:wd].reshape(wd, t // S5_CHUNK, S5_CHUNK)
    y3 = _s5_scan(u3, kk, wt, vt, rot, bsz)
    return _s5_gate(y3.reshape(wd, t), ht, d, wgt, bg, tm)


def _mlstm_kernel(h_ref, cw_ref, cb_ref, gb_ref, ng_ref, o_ref, c_ref, m_ref, buf_ref):
    q = ML_CHUNK
    nqk = 2 * ML_HEADS * ML_DQK_PAD
    wv = ML_HEADS * ML_DV
    first = pl.program_id(0) == 0

    @pl.when(first)
    def _():
        c_ref[...] = jnp.zeros(c_ref.shape, F32)
        m_ref[...] = jnp.zeros(m_ref.shape, F32)

    row = lax.broadcasted_iota(jnp.int32, (q, q), 0)
    col = lax.broadcasted_iota(jnp.int32, (q, q), 1)
    causal = col <= row
    tril = causal.astype(F32)
    lane = lax.broadcasted_iota(jnp.int32, (1, ML_WIN), 1)

    for b in range(h_ref.shape[0]):
        qk = _silu(_causal_conv(h_ref[b, :, 0:nqk], buf_ref.at[b], cw_ref, cb_ref, first))
        v = h_ref[b, :, nqk:nqk + wv]
        gates = h_ref[b, :, nqk + 3 * wv:nqk + 3 * wv + LANES] + gb_ref[...]
        lsg = -_softplus(-gates)
        bc = jnp.dot(tril, lsg, precision=HI, preferred_element_type=F32)
        gates_t = gates.T
        bc_t = bc.T

        outs = []
        for hd in range(ML_HEADS):
            st = b * ML_HEADS + hd
            odd = hd % 2
            w0 = (hd // 2) * 2 * ML_DV + (LANES if odd else 0)
            vmask = (lane >= ML_WIN - ML_DV) if odd else (lane < ML_DV)
            nlane = 0 if odd else ML_DV
            vext = jnp.where(lane == nlane, 1.0, jnp.where(vmask, v[:, w0:w0 + ML_WIN], 0.0)).astype(BF16)
            qh = (qk[:, hd * ML_DQK_PAD:(hd + 1) * ML_DQK_PAD] * (ML_DQK ** -0.5)).astype(BF16)
            kh = qk[:, (ML_HEADS + hd) * ML_DQK_PAD:(ML_HEADS + hd + 1) * ML_DQK_PAD]
            bcol = bc[:, ML_HEADS + hd:ML_HEADS + hd + 1]
            brow = bc_t[ML_HEADS + hd:ML_HEADS + hd + 1, :]
            icol = gates[:, hd:hd + 1]
            irow = gates_t[hd:hd + 1, :]
            m_st = m_ref[st, 0:1, 0:1]
            log_d = jnp.where(causal, bcol + (irow - brow), -jnp.inf)
            inter = bcol + m_st
            m_row = jnp.maximum(inter, jnp.max(log_d, -1, keepdims=True))
            dmat = jnp.exp(log_d - m_row)
            isc = jnp.exp(inter - m_row)
            s = _bdot_t1(qh, kh) * dmat
            c_st = c_ref[st]
            num = _bdot(s, vext) + isc * _bdot(qh, c_st)
            den = num[:, nlane:nlane + 1]
            hh = num / jnp.maximum(jnp.abs(den), jnp.exp(-m_row))
            hv = jnp.where(vmask, hh, 0.0)
            ms = jnp.sum(hv * hv, -1, keepdims=True) * (1.0 / ML_DV)
            outs.append(hv * lax.rsqrt(ms + EPS))
            blast = bc[q - 1:q, ML_HEADS + hd:ML_HEADS + hd + 1]
            log_w = blast - bcol + icol
            m_new = jnp.maximum(blast + m_st, jnp.max(log_w, 0, keepdims=True))
            wgt = jnp.exp(log_w - m_new)
            decay = jnp.exp(blast + m_st - m_new)
            c_ref[st] = decay * c_st + _bdot_t0(kh * wgt, vext)
            m_ref[st] = jnp.broadcast_to(m_new, m_ref.shape[1:])

        tiles = []
        for pr in range(ML_HEADS // 2):
            ev, od = outs[2 * pr], outs[2 * pr + 1]
            tiles += [ev[:, :LANES], ev[:, LANES:] + od[:, :LANES], od[:, LANES:]]
        hn = jnp.concatenate(tiles, axis=1)
        o_pre = h_ref[b, :, nqk + wv:nqk + 2 * wv]
        z = h_ref[b, :, nqk + 2 * wv:nqk + 3 * wv]
        o_ref[b] = jax.nn.sigmoid(o_pre) * (hn * ng_ref[...]) * _silu(z)


def _mlstm_mixer(h3, cw, cb, gb, ng):
    bsz, seq, n = h3.shape
    nqk = 2 * ML_HEADS * ML_DQK_PAD
    wv = ML_HEADS * ML_DV
    blk = lambda c: (0, c, 0)
    return pl.pallas_call(
        _mlstm_kernel,
        grid=(seq // ML_CHUNK,),
        in_specs=[pl.BlockSpec((bsz, ML_CHUNK, n), blk),
                  _resident((CONV_K, nqk)), _resident((1, nqk)),
                  _resident((1, LANES)), _resident((1, wv))],
        out_specs=pl.BlockSpec((bsz, ML_CHUNK, wv), blk),
        out_shape=jax.ShapeDtypeStruct((bsz, seq, wv), F32),
        scratch_shapes=[pltpu.VMEM((bsz * ML_HEADS, ML_DQK_PAD, ML_WIN), F32),
                        pltpu.VMEM((bsz * ML_HEADS, TAIL, LANES), F32),
                        pltpu.VMEM((bsz, TAIL + ML_CHUNK, nqk), F32)],
        compiler_params=pltpu.CompilerParams(dimension_semantics=("arbitrary",),
                                             vmem_limit_bytes=VMEM_LIMIT),
        name="mlstm",
    )(h3, cw, cb, gb, ng)


def _ssd_kernel(h_ref, cw_ref, cb_ref, dtb_ref, alog_ref, dsk_ref, ng_ref, o_ref, s_ref, buf_ref):
    q = SSD_CHUNK
    wx = SSD_HEADS * SSD_HEADDIM
    wbc = SSD_GROUPS * SSD_STATE
    nconv = wx + 2 * wbc
    npair = SSD_HEADS // 2
    first = pl.program_id(0) == 0

    @pl.when(first)
    def _():
        s_ref[...] = jnp.zeros(s_ref.shape, F32)

    a = -jnp.exp(alog_ref[...])
    row = lax.broadcasted_iota(jnp.int32, (q, q), 0)
    col = lax.broadcasted_iota(jnp.int32, (q, q), 1)
    causal = col <= row
    tril = causal.astype(F32)
    lane = lax.broadcasted_iota(jnp.int32, (1, LANES), 1)
    lo = lane < SSD_HEADDIM
    hi = jnp.logical_not(lo)

    for b in range(h_ref.shape[0]):
        xbc = _silu(_causal_conv(h_ref[b, :, 0:nconv], buf_ref.at[b], cw_ref, cb_ref, first))
        dt = _softplus(h_ref[b, :, nconv + wx:nconv + wx + LANES] + dtb_ref[...])
        acum = jnp.dot(tril, dt * a, precision=HI, preferred_element_type=F32)
        acum_t = acum.T
        alast = acum[q - 1:q, :]
        dec_s = jnp.exp(alast - acum)
        eac = jnp.exp(acum)
        cdec = jnp.exp(alast)

        bm = [xbc[:, wx + g * SSD_STATE:wx + (g + 1) * SSD_STATE] for g in range(SSD_GROUPS)]
        cm = [xbc[:, wx + wbc + g * SSD_STATE:wx + wbc + (g + 1) * SSD_STATE] for g in range(SSD_GROUPS)]
        cbm = [_bdot_t1(cm[g], bm[g]) for g in range(SSD_GROUPS)]

        ys = []
        for pr in range(npair):
            xp = xbc[:, pr * LANES:(pr + 1) * LANES]
            h0, h1 = 2 * pr, 2 * pr + 1
            dtx = xp * jnp.where(lo, dt[:, h0:h0 + 1], dt[:, h1:h1 + 1])
            s_prev = s_ref[b * npair + pr]
            y = dsk_ref[:, pr * LANES:(pr + 1) * LANES] * xp
            st = jnp.zeros((SSD_STATE, LANES), F32)
            for hd, msk in ((h0, lo), (h1, hi)):
                g = hd // SSD_RATIO
                xm = jnp.where(msk, dtx, 0.0).astype(BF16)
                seg = acum[:, hd:hd + 1] - acum_t[hd:hd + 1, :]
                lmat = jnp.exp(jnp.where(causal, seg, -jnp.inf))
                y = y + _bdot(cbm[g] * lmat, xm)
                st = st + _bdot_t0(bm[g] * dec_s[:, hd:hd + 1], xm)
                y = y + _bdot(cm[g] * eac[:, hd:hd + 1], jnp.where(msk, s_prev, 0.0))
            s_ref[b * npair + pr] = s_prev * jnp.where(lo, cdec[:, h0:h0 + 1], cdec[:, h1:h1 + 1]) + st
            ys.append(y)

        yz = jnp.concatenate(ys, axis=1) * _silu(h_ref[b, :, nconv:nconv + wx])
        ms = jnp.mean(yz * yz, -1, keepdims=True)
        o_ref[b] = yz * lax.rsqrt(ms + EPS) * ng_ref[...]


def _ssd_mixer(h3, cw, cb, dtb, alog, dsk, ng):
    bsz, seq, n = h3.shape
    wx = SSD_HEADS * SSD_HEADDIM
    nconv = wx + 2 * SSD_GROUPS * SSD_STATE
    blk = lambda c: (0, c, 0)
    return pl.pallas_call(
        _ssd_kernel,
        grid=(seq // SSD_CHUNK,),
        in_specs=[pl.BlockSpec((bsz, SSD_CHUNK, n), blk),
                  _resident((CONV_K, nconv)), _resident((1, nconv)),
                  _resident((1, LANES)), _resident((1, LANES)),
                  _resident((1, wx)), _resident((1, wx))],
        out_specs=pl.BlockSpec((bsz, SSD_CHUNK, wx), blk),
        out_shape=jax.ShapeDtypeStruct((bsz, seq, wx), F32),
        scratch_shapes=[pltpu.VMEM((bsz * (SSD_HEADS // 2), SSD_STATE, LANES), F32),
                        pltpu.VMEM((bsz, TAIL + SSD_CHUNK, nconv), F32)],
        compiler_params=pltpu.CompilerParams(dimension_semantics=("arbitrary",),
                                             vmem_limit_bytes=VMEM_LIMIT),
        name="ssd",
    )(h3, cw, cb, dtb, alog, dsk, ng)


def _pad_last(a, n):
    return jnp.pad(a, [(0, 0)] * (a.ndim - 1) + [(0, n - a.shape[-1])])


def _pad_heads(a, heads, width, padded):
    s = a.shape[:-1]
    a = a.reshape(s + (heads, width))
    return _pad_last(a, padded).reshape(s + (heads * padded,))


def kernel(x, w_in, w_out, ln_g, ln_b, s5_lambda_re, s5_lambda_im, s5_log_step, s5_b_re, s5_b_im, s5_c_re, s5_c_im, s5_d, s5_w_glu, s5_b_glu, ml_conv_w, ml_conv_b, ml_i_bias, ml_f_bias, ml_norm_g, ssd_conv_w, ssd_conv_b, ssd_dt_bias, ssd_a_log, ssd_d, ssd_norm_g):
    bsz, seq, dm = x.shape
    depth = w_in.shape[0]
    t = bsz * seq
    s5w = s5_d.shape[-1]
    mlw = ML_HEADS * ML_DV
    mlqk = ML_HEADS * ML_DQK
    ssdw = SSD_HEADS * SSD_HEADDIM
    ssdbc = SSD_GROUPS * SSD_STATE
    alpha = (2.0 * depth) ** 0.25
    tm = min(256, seq)

    sizes = (s5w, s5w, 2 * mlqk, mlw, ML_HEADS, ML_HEADS, mlw, mlw, ssdw + 2 * ssdbc, SSD_HEADS, ssdw)
    offs = [0]
    for sz in sizes:
        offs.append(offs[-1] + sz)
    col = lambda i: w_in[:, :, offs[i]:offs[i + 1]]
    w_s5t = jnp.swapaxes(w_in[:, :, :2 * s5w], 1, 2).astype(BF16)
    qkw = col(2)
    w_ml = jnp.concatenate([
        _pad_heads(qkw[..., :mlqk], ML_HEADS, ML_DQK, ML_DQK_PAD),
        _pad_heads(qkw[..., mlqk:], ML_HEADS, ML_DQK, ML_DQK_PAD),
        col(3), col(6), col(7),
        _pad_last(jnp.concatenate([col(4), col(5)], -1), LANES)], -1).astype(BF16)
    w_ssd = jnp.concatenate([col(8), col(10), _pad_last(col(9), LANES)], -1).astype(BF16)

    ml_cw = jnp.concatenate([_pad_heads(ml_conv_w[..., :mlqk], ML_HEADS, ML_DQK, ML_DQK_PAD),
                             _pad_heads(ml_conv_w[..., mlqk:], ML_HEADS, ML_DQK, ML_DQK_PAD)], -1)
    ml_cb = jnp.concatenate([_pad_heads(ml_conv_b[..., :mlqk], ML_HEADS, ML_DQK, ML_DQK_PAD),
                             _pad_heads(ml_conv_b[..., mlqk:], ML_HEADS, ML_DQK, ML_DQK_PAD)], -1)
    ml_gb = _pad_last(jnp.concatenate([ml_i_bias, ml_f_bias], -1), LANES)
    ssd_dtb = _pad_last(ssd_dt_bias, LANES)
    ssd_al = _pad_last(ssd_a_log, LANES)
    ssd_dsk = jnp.repeat(ssd_d, SSD_HEADDIM, axis=-1)
    w_o = w_out.astype(BF16)
    s5_wgt = jnp.swapaxes(s5_w_glu, 1, 2).astype(BF16)
    kk, wt, vt, rot = _s5_tables(s5_lambda_re, s5_lambda_im, s5_log_step, s5_b_re, s5_b_im, s5_c_re, s5_c_im)

    x2 = x.reshape(t, dm)
    for l in range(depth):
        ht_s5, h_ml, h_ssd = _inproj(x2, w_s5t[l], w_ml[l], w_ssd[l], tm)
        y_s5 = _s5_mixer(ht_s5, (kk[l], wt[l], vt[l], rot[l]), s5_d[l][:, None], s5_wgt[l],
                         s5_b_glu[l][:, None], bsz, tm)
        y_ml = _mlstm_mixer(h_ml.reshape(bsz, seq, -1), ml_cw[l], ml_cb[l][None], ml_gb[l][None],
                            ml_norm_g[l][None])
        y_ssd = _ssd_mixer(h_ssd.reshape(bsz, seq, -1), ssd_conv_w[l], ssd_conv_b[l][None], ssd_dtb[l][None],
                           ssd_al[l][None], ssd_dsk[l][None], ssd_norm_g[l][None])
        x2 = _outproj(x2, y_s5, y_ml.reshape(t, mlw), y_ssd.reshape(t, ssdw),
                      w_o[l, :s5w], w_o[l, s5w:s5w + mlw], w_o[l, s5w + mlw:],
                      ln_g[l][None], ln_b[l][None], tm, alpha)
    return x2.reshape(bsz, seq, dm)
```

```python
import functools
import math

import jax
import jax.numpy as jnp
from jax import lax
from jax.experimental import pallas as pl
from jax.experimental.pallas import tpu as pltpu

F32 = jnp.float32
BF16 = jnp.bfloat16
HI = lax.Precision.HIGHEST

LANES = 128
TAIL = 8
VMEM_LIMIT = 56 * 1024 * 1024

CONV_K = 4
EPS = 1e-5

S5_GROUP = 16
S5_STATE = 64
S5_CHUNK = LANES

ML_HEADS = 4
ML_DV = 192
ML_DQK = 96
ML_DQK_PAD = 128
ML_CHUNK = 128
ML_GROUP = 8

SSD_HEADDIM = 64
SSD_HEADS = 12
SSD_GROUPS = 4
SSD_RATIO = SSD_HEADS // SSD_GROUPS
SSD_STATE = 128
SSD_CHUNK = 128

WIN = 256
WIN_VALID = 192
assert ML_DV == WIN_VALID and SSD_RATIO * SSD_HEADDIM == WIN_VALID


def _silu(x):
    return x * jax.nn.sigmoid(x)


def _softplus(x):
    return jnp.maximum(x, 0.0) + jnp.log1p(jnp.exp(-jnp.abs(x)))


def _bdot(a, b):
    return jnp.dot(a.astype(BF16), b.astype(BF16), preferred_element_type=F32)


def _bdot_t0(a, b):
    return lax.dot_general(a.astype(BF16), b.astype(BF16), (((0,), (0,)), ((), ())),
                           preferred_element_type=F32)


def _bdot_t1(a, b):
    return lax.dot_general(a.astype(BF16), b.astype(BF16), (((1,), (1,)), ((), ())),
                           preferred_element_type=F32)


def _resident(shape):
    zeros = (0,) * len(shape)
    return pl.BlockSpec(shape, lambda *_: zeros, pipeline_mode=pl.Buffered(1))


def _window(idx):
    start = (idx // 2) * 2 * WIN_VALID + (LANES if idx % 2 else 0)
    return start, (WIN - WIN_VALID if idx % 2 else 0)


def _join_windows(wins):
    tiles = []
    for pr in range(len(wins) // 2):
        ev, od = wins[2 * pr], wins[2 * pr + 1]
        tiles += [ev[:, :LANES], ev[:, LANES:] + od[:, :LANES], od[:, LANES:]]
    return jnp.concatenate(tiles, axis=1)


def _inproj_kernel(x_ref, ws5_ref, wml_ref, wssd_ref, mcw_ref, mcb_ref, scw_ref, scb_ref,
                   os5_ref, oml_ref, ossd_ref, mtail_ref, stail_ref, mbuf_ref, sbuf_ref,
                   *, tn, tiles_per_seq, ml_segs, ssd_segs):
    tm = x_ref.shape[0]

    @pl.when(pl.program_id(0) % tiles_per_seq == 0)
    def _():
        mtail_ref[...] = jnp.zeros(mtail_ref.shape, F32)
        stail_ref[...] = jnp.zeros(stail_ref.shape, F32)

    xb = x_ref[...].astype(BF16)
    for j in range(0, os5_ref.shape[0], tn):
        os5_ref[j:j + tn, :] = lax.dot_general(ws5_ref[j:j + tn, :], xb, (((1,), (1,)), ((), ())),
                                               preferred_element_type=F32)

    def conv_silu(w_ref, o_ref, cw_ref, cb_ref, tail_ref, buf_ref, width):
        for j in range(0, width, tn):
            e = min(j + tn, width)
            r = jnp.dot(xb, w_ref[:, j:e], preferred_element_type=F32)
            buf_ref[0:TAIL, j:e] = tail_ref[:, j:e]
            buf_ref[TAIL:TAIL + tm, j:e] = r
            acc = cb_ref[:, j:e] + cw_ref[CONV_K - 1:CONV_K, j:e] * r
            for k in range(CONV_K - 1):
                off = TAIL - (CONV_K - 1) + k
                acc = acc + cw_ref[k:k + 1, j:e] * buf_ref[off:off + tm, j:e]
            tail_ref[:, j:e] = r[tm - TAIL:tm, :]
            o_ref[:, j:e] = _silu(acc)

    def plain(w_ref, o_ref, lo, hi, fn):
        for j in range(lo, hi, tn):
            e = min(j + tn, hi)
            r = jnp.dot(xb, w_ref[:, j:e], preferred_element_type=F32)
            o_ref[:, j:e] = r if fn is None else fn(r)

    conv_silu(wml_ref, oml_ref, mcw_ref, mcb_ref, mtail_ref, mbuf_ref, ml_segs[0][0])
    for lo, hi, fn in ml_segs[1:]:
        plain(wml_ref, oml_ref, lo, hi, fn)
    conv_silu(wssd_ref, ossd_ref, scw_ref, scb_ref, stail_ref, sbuf_ref, ssd_segs[0][0])
    for lo, hi, fn in ssd_segs[1:]:
        plain(wssd_ref, ossd_ref, lo, hi, fn)


def _inproj(x2, ws5t, wml, wssd, mcw, mcb, scw, scb, tm, seq):
    t, k = x2.shape
    ns5, nml, nssd = ws5t.shape[0], wml.shape[1], wssd.shape[1]
    nqk, nconv = mcw.shape[1], scw.shape[1]
    wv = ML_HEADS * ML_DV
    wx = SSD_HEADS * SSD_HEADDIM
    ml_segs = ((nqk,), (nqk, nqk + wv, None), (nqk + wv, nqk + 2 * wv, jax.nn.sigmoid),
               (nqk + 2 * wv, nqk + 3 * wv, _silu), (nqk + 3 * wv, nml, None))
    ssd_segs = ((nconv,), (nconv, nconv + wx, _silu), (nconv + wx, nssd, None))
    return pl.pallas_call(
        functools.partial(_inproj_kernel, tn=512, tiles_per_seq=seq // tm, ml_segs=ml_segs, ssd_segs=ssd_segs),
        grid=(t // tm,),
        in_specs=[pl.BlockSpec((tm, k), lambda i: (i, 0)),
                  _resident(ws5t.shape), _resident(wml.shape), _resident(wssd.shape),
                  _resident(mcw.shape), _resident(mcb.shape), _resident(scw.shape), _resident(scb.shape)],
        out_specs=[pl.BlockSpec((ns5, tm), lambda i: (0, i)),
                   pl.BlockSpec((tm, nml), lambda i: (i, 0)),
                   pl.BlockSpec((tm, nssd), lambda i: (i, 0))],
        out_shape=[jax.ShapeDtypeStruct((ns5, t), F32),
                   jax.ShapeDtypeStruct((t, nml), F32),
                   jax.ShapeDtypeStruct((t, nssd), F32)],
        scratch_shapes=[pltpu.VMEM((TAIL, nqk), F32), pltpu.VMEM((TAIL, nconv), F32),
                        pltpu.VMEM((TAIL + tm, nqk), F32), pltpu.VMEM((TAIL + tm, nconv), F32)],
        compiler_params=pltpu.CompilerParams(dimension_semantics=("arbitrary",),
                                             vmem_limit_bytes=VMEM_LIMIT),
        name="inproj",
    )(x2, ws5t, wml, wssd, mcw, mcb, scw, scb)


def _outproj_kernel(x_ref, a_ref, b_ref, c_ref, wa_ref, wb_ref, wc_ref, g_ref, bb_ref, o_ref, *, alpha):
    out = (_bdot_t0(a_ref[...], wa_ref[...]) + _bdot(b_ref[...], wb_ref[...])
           + _bdot(c_ref[...], wc_ref[...]))
    y = alpha * x_ref[...] + out
    mu = jnp.mean(y, -1, keepdims=True)
    d = y - mu
    var = jnp.mean(d * d, -1, keepdims=True)
    o_ref[...] = d * lax.rsqrt(var + EPS) * g_ref[...] + bb_ref[...]


def _outproj(x2, mat, mb, mc, wa, wb, wc, g, b, tm, alpha):
    t, d = x2.shape
    row = lambda i: (i, 0)
    return pl.pallas_call(
        functools.partial(_outproj_kernel, alpha=alpha),
        grid=(t // tm,),
        in_specs=[pl.BlockSpec((tm, d), row),
                  pl.BlockSpec((mat.shape[0], tm), lambda i: (0, i)),
                  pl.BlockSpec((tm, mb.shape[1]), row),
                  pl.BlockSpec((tm, mc.shape[1]), row),
                  _resident(wa.shape), _resident(wb.shape), _resident(wc.shape),
                  _resident((1, d)), _resident((1, d))],
        out_specs=pl.BlockSpec((tm, d), row),
        out_shape=jax.ShapeDtypeStruct((t, d), F32),
        compiler_params=pltpu.CompilerParams(dimension_semantics=("arbitrary",),
                                             vmem_limit_bytes=VMEM_LIMIT),
        name="outproj_ln",
    )(x2, mat, mb, mc, wa, wb, wc, g, b)


def _s5_scan_kernel(u_ref, kk_ref, wt_ref, vt_ref, rot_ref, y_ref, tt_ref, sloc_ref, sin_ref, *, bsz, nchunk):
    q = S5_CHUNK
    p2 = 2 * S5_STATE
    srow = lax.broadcasted_iota(jnp.int32, (q, q), 0)
    tcol = lax.broadcasted_iota(jnp.int32, (q, q), 1)
    upper = tcol >= srow

    def build(ci, carry):
        for co in range(S5_GROUP):
            lag = kk_ref[0, pl.ds(ci * S5_GROUP + co, 1), :]
            blk = pltpu.roll(jnp.broadcast_to(lag, (q, q)), 0, 1, stride=1, stride_axis=0)
            tt_ref[pl.ds(pl.multiple_of(ci * q, q), q), co * q:(co + 1) * q] = (
                jnp.where(upper, blk, 0.0).astype(BF16))
        return carry

    lax.fori_loop(0, S5_GROUP, build, 0)

    u = jnp.concatenate([u_ref[ci] for ci in range(S5_GROUP)], axis=1).astype(BF16)
    sloc_ref[...] = jnp.dot(u, wt_ref[0], preferred_element_type=F32)
    a1 = rot_ref[0, 0:1, :]
    a2 = rot_ref[0, 1:2, :]

    def body(c, states):
        nxt = []
        for b in range(bsz):
            r = pl.ds(b * nchunk + c, 1)
            s = states[b]
            sin_ref[r, :] = s
            swapped = jnp.concatenate([s[:, p2:], s[:, :p2]], axis=1)
            nxt.append(a1 * s + a2 * swapped + sloc_ref[r, :])
        return tuple(nxt)

    lax.fori_loop(0, nchunk, body, tuple(jnp.zeros((1, 2 * p2), F32) for _ in range(bsz)))
    y = (jnp.dot(u, tt_ref[...], preferred_element_type=F32)
         + _bdot(sin_ref[:, 0:p2], vt_ref[0]))
    for co in range(S5_GROUP):
        y_ref[co] = y[:, co * q:(co + 1) * q]


def _s5_scan(u3, kk, wt, vt, rot, bsz):
    w, n, q = u3.shape
    g = w // S5_GROUP
    qc = S5_GROUP * q
    p2 = 2 * S5_STATE
    grp = lambda i: (i, 0, 0)
    return pl.pallas_call(
        functools.partial(_s5_scan_kernel, bsz=bsz, nchunk=n // bsz),
        grid=(g,),
        in_specs=[pl.BlockSpec((S5_GROUP, n, q), grp),
                  pl.BlockSpec((1, S5_GROUP * S5_GROUP, q), grp),
                  pl.BlockSpec((1, qc, 2 * p2), grp), pl.BlockSpec((1, p2, qc), grp),
                  pl.BlockSpec((1, 2, 2 * p2), grp)],
        out_specs=pl.BlockSpec((S5_GROUP, n, q), grp),
        out_shape=jax.ShapeDtypeStruct((w, n, q), F32),
        scratch_shapes=[pltpu.VMEM((qc, qc), BF16),
                        pltpu.VMEM((n, 2 * p2), F32), pltpu.VMEM((n, 2 * p2), F32)],
        compiler_params=pltpu.CompilerParams(dimension_semantics=("arbitrary",),
                                             vmem_limit_bytes=VMEM_LIMIT),
        name="s5_scan",
    )(u3, kk, wt, vt, rot)


def _s5_gate_kernel(y_ref, u_ref, z_ref, d_ref, w_ref, b_ref, o_ref):
    y = y_ref[...] + d_ref[...] * u_ref[...]
    c = math.sqrt(2.0 / math.pi)
    g = 0.5 * y * (1.0 + jnp.tanh(c * (y + 0.044715 * (y * y * y))))
    gate = jax.nn.sigmoid(_bdot(w_ref[...], g) + b_ref[...])
    o_ref[...] = g * gate * _silu(z_ref[...])


def _s5_gate(yt, ht, d, wgt, b, tm):
    wd, t = yt.shape
    return pl.pallas_call(
        _s5_gate_kernel,
        grid=(t // tm,),
        in_specs=[pl.BlockSpec((wd, tm), lambda i: (0, i)),
                  pl.BlockSpec((wd, tm), lambda i: (0, i)),
                  pl.BlockSpec((wd, tm), lambda i: (1, i)),
                  _resident((wd, 1)), _resident((wd, wd)), _resident((wd, 1))],
        out_specs=pl.BlockSpec((wd, tm), lambda i: (0, i)),
        out_shape=jax.ShapeDtypeStruct((wd, t), F32),
        compiler_params=pltpu.CompilerParams(dimension_semantics=("arbitrary",),
                                             vmem_limit_bytes=VMEM_LIMIT),
        name="s5_gate",
    )(yt, ht, ht, d, wgt, b)


def _s5_tables(lam_re, lam_im, log_step, b_re, b_im, c_re, c_im):
    q = S5_CHUNK
    dl, g = lam_re.shape[0], lam_re.shape[1]
    step = jnp.exp(log_step)[..., None]
    zr, zi = lam_re * step, lam_im * step
    er = jnp.exp(zr)
    lbr, lbi = er * jnp.cos(zi), er * jnp.sin(zi)
    nr, ni = lbr - 1.0, lbi
    den = lam_re * lam_re + lam_im * lam_im
    fr = (nr * lam_re + ni * lam_im) / den
    fi = (ni * lam_re - nr * lam_im) / den
    bbr = fr[..., None] * b_re - fi[..., None] * b_im
    bbi = fr[..., None] * b_im + fi[..., None] * b_re
    dd = jnp.arange(q + 1, dtype=F32)[:, None, None, None]
    pm = jnp.exp(dd * zr[None])
    pr, pi = pm * jnp.cos(dd * zi[None]), pm * jnp.sin(dd * zi[None])
    cpr = c_re[None] * pr[:, :, :, None, :] - c_im[None] * pi[:, :, :, None, :]
    cpi = c_re[None] * pi[:, :, :, None, :] + c_im[None] * pr[:, :, :, None, :]
    cpc = jnp.concatenate([cpr, -cpi], axis=4)
    kk = jnp.einsum('qdgop,dgpi->dgioq', cpc[:q], jnp.concatenate([bbr, bbi], axis=2), precision=HI)
    kk = kk.reshape(dl, g, S5_GROUP * S5_GROUP, q)
    rr, ri = pr[:q][::-1], pi[:q][::-1]
    wr = rr[..., None] * bbr[None] - ri[..., None] * bbi[None]
    wi = rr[..., None] * bbi[None] + ri[..., None] * bbr[None]
    wt = jnp.concatenate([wr, wi, wi, wr], axis=3)
    wt = wt.transpose(1, 2, 4, 0, 3).reshape(dl, g, S5_GROUP * q, 4 * S5_STATE)
    vt = cpc[1:].transpose(1, 2, 4, 3, 0).reshape(dl, g, 2 * S5_STATE, S5_GROUP * q)
    rot = jnp.stack([jnp.concatenate([pr[q]] * 4, -1),
                     jnp.concatenate([-pi[q], pi[q], pi[q], -pi[q]], -1)], axis=2)
    return kk, wt.astype(BF16), vt.astype(BF16), rot


def _s5_mixer(ht, tabs, d, wgt, bg, bsz, tm):
    kk, wt, vt, rot = tabs
    wd = ht.shape[0] // 2
    t = ht.shape[1]
    u3 = ht[:wd].reshape(wd, t // S5_CHUNK, S5_CHUNK)
    y3 = _s5_scan(u3, kk, wt, vt, rot, bsz)
    return _s5_gate(y3.reshape(wd, t), ht, d, wgt, bg, tm)


def _mlstm_kernel(h_ref, gb_ref, ng_ref, o_ref, c_ref, m_ref, *, group):
    q = ML_CHUNK
    nqk = 2 * ML_HEADS * ML_DQK_PAD
    wv = ML_HEADS * ML_DV
    bsz = h_ref.shape[0]

    @pl.when(pl.program_id(0) == 0)
    def _():
        c_ref[...] = jnp.zeros(c_ref.shape, F32)
        m_ref[...] = jnp.zeros(m_ref.shape, F32)

    row = lax.broadcasted_iota(jnp.int32, (q, q), 0)
    col = lax.broadcasted_iota(jnp.int32, (q, q), 1)
    causal = col <= row
    tril = causal.astype(F32)
    lane = lax.broadcasted_iota(jnp.int32, (1, WIN), 1)
    one_b, zero_b = jnp.ones((), BF16), jnp.zeros((), BF16)

    gates, bc, gates_t, bc_t = [], [], [], []
    for b in range(bsz):
        g = h_ref[b, :, nqk + 3 * wv:nqk + 3 * wv + LANES] + gb_ref[...]
        c = jnp.dot(tril, -_softplus(-g), precision=HI, preferred_element_type=F32)
        gates.append(g)
        bc.append(c)
        gates_t.append(g.T)
        bc_t.append(c.T)

    hn = {}
    for par in (0, 1):
        allp = [(b, hd) for hd in range(par, ML_HEADS, 2) for b in range(bsz)]
        first_valid = _window(par)[1]
        vmask = (lane >= first_valid) if first_valid else (lane < WIN_VALID)
        nlane = 0 if first_valid else WIN_VALID
        for g0 in range(0, len(allp), group):
            pairs = allp[g0:g0 + group]
            idx = [b * ML_HEADS + hd for b, hd in pairs]
            n = len(pairs)

            def stack(fn):
                return jnp.stack([fn(b, hd) for b, hd in pairs])

            vext = [jnp.where(lane == nlane, one_b,
                              jnp.where(vmask, h_ref[b, :, nqk + _window(hd)[0]:nqk + _window(hd)[0] + WIN].astype(BF16),
                                        zero_b)) for b, hd in pairs]
            qh = [(h_ref[b, :, hd * ML_DQK_PAD:(hd + 1) * ML_DQK_PAD] * (ML_DQK ** -0.5)).astype(BF16)
                  for b, hd in pairs]
            kh = stack(lambda b, hd: h_ref[b, :, (ML_HEADS + hd) * ML_DQK_PAD:(ML_HEADS + hd + 1) * ML_DQK_PAD])
            bcol = stack(lambda b, hd: bc[b][:, ML_HEADS + hd:ML_HEADS + hd + 1])
            brow = stack(lambda b, hd: bc_t[b][ML_HEADS + hd:ML_HEADS + hd + 1, :])
            icol = stack(lambda b, hd: gates[b][:, hd:hd + 1])
            irow = stack(lambda b, hd: gates_t[b][hd:hd + 1, :])
            m_st = jnp.stack([m_ref[i, 0:1, 0:1] for i in idx])
            c_st = [c_ref[i] for i in idx]

            log_d = jnp.where(causal, bcol + (irow - brow), -jnp.inf)
            inter = bcol + m_st
            m_row = jnp.maximum(inter, jnp.max(log_d, -1, keepdims=True))
            dmat = jnp.exp(log_d - m_row)
            isc = jnp.exp(inter - m_row)
            s = (jnp.stack([_bdot_t1(qh[i], kh[i]) for i in range(n)]) * dmat).astype(BF16)
            num = (jnp.stack([jnp.dot(s[i], vext[i], preferred_element_type=F32) for i in range(n)])
                   + isc * jnp.stack([_bdot(qh[i], c_st[i]) for i in range(n)]))
            den = num[:, :, nlane:nlane + 1]
            hh = num * (1.0 / jnp.maximum(jnp.abs(den), jnp.exp(-m_row)))
            hv = jnp.where(lane == nlane, 0.0, hh)
            ms = jnp.sum(hv * hv, -1, keepdims=True) * (1.0 / ML_DV)
            hnorm = hv * lax.rsqrt(ms + EPS)

            blast = bcol[:, q - 1:q, :]
            log_w = blast - bcol + icol
            m_new = jnp.maximum(blast + m_st, jnp.max(log_w, 1, keepdims=True))
            kw = (kh * jnp.exp(log_w - m_new)).astype(BF16)
            decay = jnp.exp(blast + m_st - m_new)
            for i in range(n):
                c_ref[idx[i]] = decay[i] * c_st[i] + _bdot_t0(kw[i], vext[i])
                m_ref[idx[i]] = jnp.broadcast_to(m_new[i], m_ref.shape[1:])
                hn[pairs[i]] = hnorm[i]

    for b in range(bsz):
        joined = _join_windows([hn[(b, hd)] for hd in range(ML_HEADS)])
        o_sig = h_ref[b, :, nqk + wv:nqk + 2 * wv]
        z_act = h_ref[b, :, nqk + 2 * wv:nqk + 3 * wv]
        o_ref[b] = o_sig * (joined * ng_ref[...]) * z_act


def _mlstm_mixer(h3, gb, ng):
    bsz, seq, n = h3.shape
    wv = ML_HEADS * ML_DV
    blk = lambda c: (0, c, 0)
    return pl.pallas_call(
        functools.partial(_mlstm_kernel, group=ML_GROUP),
        grid=(seq // ML_CHUNK,),
        in_specs=[pl.BlockSpec((bsz, ML_CHUNK, n), blk), _resident((1, LANES)), _resident((1, wv))],
        out_specs=pl.BlockSpec((bsz, ML_CHUNK, wv), blk),
        out_shape=jax.ShapeDtypeStruct((bsz, seq, wv), F32),
        scratch_shapes=[pltpu.VMEM((bsz * ML_HEADS, ML_DQK_PAD, WIN), F32),
                        pltpu.VMEM((bsz * ML_HEADS, TAIL, LANES), F32)],
        compiler_params=pltpu.CompilerParams(dimension_semantics=("arbitrary",),
                                             vmem_limit_bytes=VMEM_LIMIT),
        name="mlstm",
    )(h3, gb, ng)


def _ssd_kernel(h_ref, dtb_ref, alog_ref, dsk_ref, ng_ref, o_ref, s_ref):
    q = SSD_CHUNK
    wx = SSD_HEADS * SSD_HEADDIM
    wbc = SSD_GROUPS * SSD_STATE
    nconv = wx + 2 * wbc
    bsz = h_ref.shape[0]

    @pl.when(pl.program_id(0) == 0)
    def _():
        s_ref[...] = jnp.zeros(s_ref.shape, F32)

    a = -jnp.exp(alog_ref[...])
    row = lax.broadcasted_iota(jnp.int32, (q, q), 0)
    col = lax.broadcasted_iota(jnp.int32, (q, q), 1)
    causal = col <= row
    tril = causal.astype(F32)
    lane = lax.broadcasted_iota(jnp.int32, (1, WIN), 1)
    lo_half = lane[:, :LANES] < SSD_HEADDIM
    zero_b = jnp.zeros((), BF16)

    def per_channel(hcols):
        return jnp.concatenate([jnp.where(lo_half, hcols[..., 2 * p:2 * p + 1], hcols[..., 2 * p + 1:2 * p + 2])
                                for p in range(SSD_HEADS // 2)], axis=-1)

    x = h_ref[:, :, 0:wx]
    dt = _softplus(h_ref[:, :, nconv + wx:nconv + wx + LANES] + dtb_ref[...])
    dta = dt * a
    acum = jnp.stack([jnp.dot(tril, dta[b], precision=HI, preferred_element_type=F32)
                      for b in range(bsz)])
    acum_t = [acum[b].T for b in range(bsz)]
    alast = acum[:, q - 1:q, :]
    eac_c = per_channel(jnp.exp(acum))
    dtx = x * per_channel(dt)
    dtx_dec = dtx * per_channel(jnp.exp(alast - acum))

    ywins = [[] for _ in range(bsz)]
    for g in range(SSD_GROUPS):
        w0, first_valid = _window(g)
        win = slice(w0, w0 + WIN)
        gmask = (lane >= first_valid) if first_valid else (lane < WIN_VALID)
        bm = [h_ref[b, :, wx + g * SSD_STATE:wx + (g + 1) * SSD_STATE].astype(BF16) for b in range(bsz)]
        cm = [h_ref[b, :, wx + wbc + g * SSD_STATE:wx + wbc + (g + 1) * SSD_STATE].astype(BF16) for b in range(bsz)]
        cbm = jnp.stack([_bdot_t1(cm[b], bm[b]) for b in range(bsz)])
        s_prev = [s_ref[b * SSD_GROUPS + g] for b in range(bsz)]
        xw = dtx[:, :, win].astype(BF16)
        xdec = jnp.where(gmask, dtx_dec[:, :, win].astype(BF16), zero_b)
        heads = [g * SSD_RATIO + k for k in range(SSD_RATIO)]
        acol = jnp.concatenate([acum[:, :, hd:hd + 1] for hd in heads], axis=0)
        arow = jnp.stack([acum_t[b][hd:hd + 1, :] for hd in heads for b in range(bsz)])
        lmat = jnp.exp(jnp.where(causal, acol - arow, -jnp.inf))
        mmat = (jnp.concatenate([cbm] * SSD_RATIO, axis=0) * lmat).astype(BF16)
        yoff = jnp.stack([_bdot(cm[b], s_prev[b]) for b in range(bsz)]) * eac_c[:, :, win]
        for b in range(bsz):
            y = yoff[b]
            for k in range(SSD_RATIO):
                lo = first_valid + k * SSD_HEADDIM
                hmask = (lane >= lo) & (lane < lo + SSD_HEADDIM)
                y = y + jnp.dot(mmat[k * bsz + b], jnp.where(hmask, xw[b], zero_b), preferred_element_type=F32)
            ywins[b].append(y)
            s_ref[b * SSD_GROUPS + g] = s_prev[b] * eac_c[b, q - 1:q, win] + _bdot_t0(bm[b], xdec[b])

    for b in range(bsz):
        yz = (_join_windows(ywins[b]) + dsk_ref[...] * x[b]) * h_ref[b, :, nconv:nconv + wx]
        ms = jnp.mean(yz * yz, -1, keepdims=True)
        o_ref[b] = yz * lax.rsqrt(ms + EPS) * ng_ref[...]


def _ssd_mixer(h3, dtb, alog, dsk, ng):
    bsz, seq, n = h3.shape
    wx = SSD_HEADS * SSD_HEADDIM
    blk = lambda c: (0, c, 0)
    return pl.pallas_call(
        _ssd_kernel,
        grid=(seq // SSD_CHUNK,),
        in_specs=[pl.BlockSpec((bsz, SSD_CHUNK, n), blk),
                  _resident((1, LANES)), _resident((1, LANES)),
                  _resident((1, wx)), _resident((1, wx))],
        out_specs=pl.BlockSpec((bsz, SSD_CHUNK, wx), blk),
        out_shape=jax.ShapeDtypeStruct((bsz, seq, wx), F32),
        scratch_shapes=[pltpu.VMEM((bsz * SSD_GROUPS, SSD_STATE, WIN), F32)],
        compiler_params=pltpu.CompilerParams(dimension_semantics=("arbitrary",),
                                             vmem_limit_bytes=VMEM_LIMIT),
        name="ssd",
    )(h3, dtb, alog, dsk, ng)


def _pad_last(a, n):
    return jnp.pad(a, [(0, 0)] * (a.ndim - 1) + [(0, n - a.shape[-1])])


def _pad_heads(a, heads, width, padded):
    s = a.shape[:-1]
    a = a.reshape(s + (heads, width))
    return _pad_last(a, padded).reshape(s + (heads * padded,))


def kernel(x, w_in, w_out, ln_g, ln_b, s5_lambda_re, s5_lambda_im, s5_log_step, s5_b_re, s5_b_im, s5_c_re, s5_c_im, s5_d, s5_w_glu, s5_b_glu, ml_conv_w, ml_conv_b, ml_i_bias, ml_f_bias, ml_norm_g, ssd_conv_w, ssd_conv_b, ssd_dt_bias, ssd_a_log, ssd_d, ssd_norm_g):
    bsz, seq, dm = x.shape
    depth = w_in.shape[0]
    t = bsz * seq
    s5w = s5_d.shape[-1]
    mlw = ML_HEADS * ML_DV
    mlqk = ML_HEADS * ML_DQK
    ssdw = SSD_HEADS * SSD_HEADDIM
    ssdbc = SSD_GROUPS * SSD_STATE
    alpha = (2.0 * depth) ** 0.25
    tm = min(256, seq)

    sizes = (s5w, s5w, 2 * mlqk, mlw, ML_HEADS, ML_HEADS, mlw, mlw, ssdw + 2 * ssdbc, SSD_HEADS, ssdw)
    offs = [0]
    for sz in sizes:
        offs.append(offs[-1] + sz)
    col = lambda i: w_in[:, :, offs[i]:offs[i + 1]]
    w_s5t = jnp.swapaxes(w_in[:, :, :2 * s5w], 1, 2).astype(BF16)
    qkw = col(2)
    w_ml = jnp.concatenate([
        _pad_heads(qkw[..., :mlqk], ML_HEADS, ML_DQK, ML_DQK_PAD),
        _pad_heads(qkw[..., mlqk:], ML_HEADS, ML_DQK, ML_DQK_PAD),
        col(3), col(6), col(7),
        _pad_last(jnp.concatenate([col(4), col(5)], -1), LANES)], -1).astype(BF16)
    w_ssd = jnp.concatenate([col(8), col(10), _pad_last(col(9), LANES)], -1).astype(BF16)

    ml_cw = jnp.concatenate([_pad_heads(ml_conv_w[..., :mlqk], ML_HEADS, ML_DQK, ML_DQK_PAD),
                             _pad_heads(ml_conv_w[..., mlqk:], ML_HEADS, ML_DQK, ML_DQK_PAD)], -1)
    ml_cb = jnp.concatenate([_pad_heads(ml_conv_b[..., :mlqk], ML_HEADS, ML_DQK, ML_DQK_PAD),
                             _pad_heads(ml_conv_b[..., mlqk:], ML_HEADS, ML_DQK, ML_DQK_PAD)], -1)
    ml_gb = _pad_last(jnp.concatenate([ml_i_bias, ml_f_bias], -1), LANES)
    ssd_dtb = _pad_last(ssd_dt_bias, LANES)
    ssd_al = _pad_last(ssd_a_log, LANES)
    ssd_dsk = jnp.repeat(ssd_d, SSD_HEADDIM, axis=-1)
    w_o = w_out.astype(BF16)
    s5_wgt = jnp.swapaxes(s5_w_glu, 1, 2).astype(BF16)
    kk, wt, vt, rot = _s5_tables(s5_lambda_re, s5_lambda_im, s5_log_step, s5_b_re, s5_b_im, s5_c_re, s5_c_im)

    x2 = x.reshape(t, dm)
    for l in range(depth):
        ht_s5, h_ml, h_ssd = _inproj(x2, w_s5t[l], w_ml[l], w_ssd[l], ml_cw[l], ml_cb[l][None],
                                     ssd_conv_w[l], ssd_conv_b[l][None], tm, seq)
        y_s5 = _s5_mixer(ht_s5, (kk[l], wt[l], vt[l], rot[l]), s5_d[l][:, None], s5_wgt[l],
                         s5_b_glu[l][:, None], bsz, tm)
        y_ml = _mlstm_mixer(h_ml.reshape(bsz, seq, -1), ml_gb[l][None], ml_norm_g[l][None])
        y_ssd = _ssd_mixer(h_ssd.reshape(bsz, seq, -1), ssd_dtb[l][None], ssd_al[l][None],
                           ssd_dsk[l][None], ssd_norm_g[l][None])
        x2 = _outproj(x2, y_s5, y_ml.reshape(t, mlw), y_ssd.reshape(t, ssdw),
                      w_o[l, :s5w], w_o[l, s5w:s5w + mlw], w_o[l, s5w + mlw:],
                      ln_g[l][None], ln_b[l][None], tm, alpha)
    return x2.reshape(bsz, seq, dm)
```

```python
import functools
import math

import jax
import jax.numpy as jnp
from jax import lax
from jax.experimental import pallas as pl
from jax.experimental.pallas import tpu as pltpu

F32 = jnp.float32
BF16 = jnp.bfloat16
HI = lax.Precision.HIGHEST

LANES = 128
TAIL = 8
VMEM_LIMIT = 56 * 1024 * 1024

CONV_K = 4
EPS = 1e-5

S5_GROUP = 16
S5_STATE = 64
S5_CHUNK = LANES

ML_HEADS = 4
ML_DV = 192
ML_DQK = 96
ML_DQK_PAD = 128
ML_CHUNK = 128
ML_GROUP = 8

SSD_HEADDIM = 64
SSD_HEADS = 12
SSD_GROUPS = 4
SSD_RATIO = SSD_HEADS // SSD_GROUPS
SSD_STATE = 128
SSD_CHUNK = 128

WIN = 256
WIN_VALID = 192
assert ML_DV == WIN_VALID and SSD_RATIO * SSD_HEADDIM == WIN_VALID


def _silu(x):
    return x * jax.nn.sigmoid(x)


def _softplus(x):
    return jnp.maximum(x, 0.0) + jnp.log1p(jnp.exp(-jnp.abs(x)))


def _bdot(a, b):
    return jnp.dot(a.astype(BF16), b.astype(BF16), preferred_element_type=F32)


def _bdot_t0(a, b):
    return lax.dot_general(a.astype(BF16), b.astype(BF16), (((0,), (0,)), ((), ())),
                           preferred_element_type=F32)


def _bdot_t1(a, b):
    return lax.dot_general(a.astype(BF16), b.astype(BF16), (((1,), (1,)), ((), ())),
                           preferred_element_type=F32)


def _resident(shape):
    zeros = (0,) * len(shape)
    return pl.BlockSpec(shape, lambda *_: zeros, pipeline_mode=pl.Buffered(1))


def _window(idx):
    start = (idx // 2) * 2 * WIN_VALID + (LANES if idx % 2 else 0)
    return start, (WIN - WIN_VALID if idx % 2 else 0)


def _join_windows(wins):
    tiles = []
    for pr in range(len(wins) // 2):
        ev, od = wins[2 * pr], wins[2 * pr + 1]
        tiles += [ev[:, :LANES], ev[:, LANES:] + od[:, :LANES], od[:, LANES:]]
    return jnp.concatenate(tiles, axis=1)


def _inproj_kernel(x_ref, ws5_ref, wml_ref, wssd_ref, mcw_ref, mcb_ref, scw_ref, scb_ref,
                   os5_ref, oml_ref, ossd_ref, mtail_ref, stail_ref, mbuf_ref, sbuf_ref,
                   *, tn, tiles_per_seq, ml_segs, ssd_segs):
    tm = x_ref.shape[0]

    @pl.when(pl.program_id(0) % tiles_per_seq == 0)
    def _():
        mtail_ref[...] = jnp.zeros(mtail_ref.shape, F32)
        stail_ref[...] = jnp.zeros(stail_ref.shape, F32)

    xb = x_ref[...].astype(BF16)
    for j in range(0, os5_ref.shape[0], tn):
        os5_ref[j:j + tn, :] = lax.dot_general(ws5_ref[j:j + tn, :], xb, (((1,), (1,)), ((), ())),
                                               preferred_element_type=F32)

    def conv_silu(w_ref, o_ref, cw_ref, cb_ref, tail_ref, buf_ref, width):
        for j in range(0, width, tn):
            e = min(j + tn, width)
            r = jnp.dot(xb, w_ref[:, j:e], preferred_element_type=F32)
            buf_ref[0:TAIL, j:e] = tail_ref[:, j:e]
            buf_ref[TAIL:TAIL + tm, j:e] = r
            acc = cb_ref[:, j:e] + cw_ref[CONV_K - 1:CONV_K, j:e] * r
            for k in range(CONV_K - 1):
                off = TAIL - (CONV_K - 1) + k
                acc = acc + cw_ref[k:k + 1, j:e] * buf_ref[off:off + tm, j:e]
            tail_ref[:, j:e] = r[tm - TAIL:tm, :]
            o_ref[:, j:e] = _silu(acc)

    def plain(w_ref, o_ref, lo, hi, fn):
        for j in range(lo, hi, tn):
            e = min(j + tn, hi)
            r = jnp.dot(xb, w_ref[:, j:e], preferred_element_type=F32)
            o_ref[:, j:e] = r if fn is None else fn(r)

    conv_silu(wml_ref, oml_ref, mcw_ref, mcb_ref, mtail_ref, mbuf_ref, ml_segs[0][0])
    for lo, hi, fn in ml_segs[1:]:
        plain(wml_ref, oml_ref, lo, hi, fn)
    conv_silu(wssd_ref, ossd_ref, scw_ref, scb_ref, stail_ref, sbuf_ref, ssd_segs[0][0])
    for lo, hi, fn in ssd_segs[1:]:
        plain(wssd_ref, ossd_ref, lo, hi, fn)


def _inproj(x2, ws5t, wml, wssd, mcw, mcb, scw, scb, tm, seq):
    t, k = x2.shape
    ns5, nml, nssd = ws5t.shape[0], wml.shape[1], wssd.shape[1]
    nqk, nconv = mcw.shape[1], scw.shape[1]
    wv = ML_HEADS * ML_DV
    wx = SSD_HEADS * SSD_HEADDIM
    ml_segs = ((nqk,), (nqk, nqk + wv, None), (nqk + wv, nqk + 2 * wv, jax.nn.sigmoid),
               (nqk + 2 * wv, nqk + 3 * wv, _silu), (nqk + 3 * wv, nml, None))
    ssd_segs = ((nconv,), (nconv, nconv + wx, _silu), (nconv + wx, nssd, None))
    return pl.pallas_call(
        functools.partial(_inproj_kernel, tn=512, tiles_per_seq=seq // tm, ml_segs=ml_segs, ssd_segs=ssd_segs),
        grid=(t // tm,),
        in_specs=[pl.BlockSpec((tm, k), lambda i: (i, 0)),
                  _resident(ws5t.shape), _resident(wml.shape), _resident(wssd.shape),
                  _resident(mcw.shape), _resident(mcb.shape), _resident(scw.shape), _resident(scb.shape)],
        out_specs=[pl.BlockSpec((ns5, tm), lambda i: (0, i)),
                   pl.BlockSpec((tm, nml), lambda i: (i, 0)),
                   pl.BlockSpec((tm, nssd), lambda i: (i, 0))],
        out_shape=[jax.ShapeDtypeStruct((ns5, t), F32),
                   jax.ShapeDtypeStruct((t, nml), F32),
                   jax.ShapeDtypeStruct((t, nssd), F32)],
        scratch_shapes=[pltpu.VMEM((TAIL, nqk), F32), pltpu.VMEM((TAIL, nconv), F32),
                        pltpu.VMEM((TAIL + tm, nqk), F32), pltpu.VMEM((TAIL + tm, nconv), F32)],
        compiler_params=pltpu.CompilerParams(dimension_semantics=("arbitrary",),
                                             vmem_limit_bytes=VMEM_LIMIT),
        name="inproj",
    )(x2, ws5t, wml, wssd, mcw, mcb, scw, scb)


def _outproj_kernel(x_ref, a_ref, b_ref, c_ref, wa_ref, wb_ref, wc_ref, g_ref, bb_ref, o_ref, *, alpha):
    out = (_bdot_t0(a_ref[...], wa_ref[...]) + _bdot(b_ref[...], wb_ref[...])
           + _bdot(c_ref[...], wc_ref[...]))
    y = alpha * x_ref[...] + out
    mu = jnp.mean(y, -1, keepdims=True)
    d = y - mu
    var = jnp.mean(d * d, -1, keepdims=True)
    o_ref[...] = d * lax.rsqrt(var + EPS) * g_ref[...] + bb_ref[...]


def _outproj(x2, mat, mb, mc, wa, wb, wc, g, b, tm, alpha):
    t, d = x2.shape
    row = lambda i: (i, 0)
    return pl.pallas_call(
        functools.partial(_outproj_kernel, alpha=alpha),
        grid=(t // tm,),
        in_specs=[pl.BlockSpec((tm, d), row),
                  pl.BlockSpec((mat.shape[0], tm), lambda i: (0, i)),
                  pl.BlockSpec((tm, mb.shape[1]), row),
                  pl.BlockSpec((tm, mc.shape[1]), row),
                  _resident(wa.shape), _resident(wb.shape), _resident(wc.shape),
                  _resident((1, d)), _resident((1, d))],
        out_specs=pl.BlockSpec((tm, d), row),
        out_shape=jax.ShapeDtypeStruct((t, d), F32),
        compiler_params=pltpu.CompilerParams(dimension_semantics=("arbitrary",),
                                             vmem_limit_bytes=VMEM_LIMIT),
        name="outproj_ln",
    )(x2, mat, mb, mc, wa, wb, wc, g, b)


def _s5_scan_kernel(u_ref, ptab_ref, stab_ref, ctab_ref, bbrow_ref, bbt_ref, rot_ref, y_ref,
                    tt_ref, kk_ref, sloc_ref, sin_ref, *, bsz, nchunk):
    q = S5_CHUNK
    p2 = 2 * S5_STATE
    ng = S5_GROUP
    srow = lax.broadcasted_iota(jnp.int32, (q, q), 0)
    tcol = lax.broadcasted_iota(jnp.int32, (q, q), 1)
    upper = tcol >= srow

    ctab = ctab_ref[0]

    def c_lam(co, a, b):
        return ctab[:, co:co + 1] * a + ctab[:, ng + co:ng + co + 1] * b

    cp0 = jnp.concatenate([c_lam(co, ptab_ref[0, 0], ptab_ref[0, 1]) for co in range(ng)], axis=1)
    vt = jnp.concatenate([c_lam(co, ptab_ref[0, 2], ptab_ref[0, 3]) for co in range(ng)], axis=1).astype(BF16)
    kk = jnp.dot(bbt_ref[0], cp0, precision=HI, preferred_element_type=F32)
    for co in range(ng):
        kk_ref[co] = kk[:, co * q:(co + 1) * q]

    def build(ci, carry):
        for co in range(ng):
            lag = kk_ref[co, pl.ds(ci, 1), :]
            blk = pltpu.roll(jnp.broadcast_to(lag, (q, q)), 0, 1, stride=1, stride_axis=0)
            tt_ref[pl.ds(pl.multiple_of(ci * q, q), q), co * q:(co + 1) * q] = (
                jnp.where(upper, blk, 0.0).astype(BF16))
        return carry

    lax.fori_loop(0, ng, build, 0)

    u3 = u_ref[...].reshape(ng, bsz * nchunk, q)
    u = jnp.concatenate([u3[ci] for ci in range(ng)], axis=1).astype(BF16)
    rs, ims = stab_ref[0, 0], stab_ref[0, 1]
    wt = jnp.concatenate(
        [(rs * bbrow_ref[0, ci:ci + 1, :] + ims * bbrow_ref[0, ng + ci:ng + ci + 1, :]).astype(BF16)
         for ci in range(ng)], axis=0)
    sloc_ref[...] = jnp.dot(u, wt, preferred_element_type=F32)
    a1 = rot_ref[0, 0:1, :]
    a2 = rot_ref[0, 1:2, :]

    def body(c, states):
        nxt = []
        for b in range(bsz):
            r = pl.ds(b * nchunk + c, 1)
            s = states[b]
            sin_ref[r, :] = s
            swapped = jnp.concatenate([s[:, p2:], s[:, :p2]], axis=1)
            nxt.append(a1 * s + a2 * swapped + sloc_ref[r, :])
        return tuple(nxt)

    lax.fori_loop(0, nchunk, body, tuple(jnp.zeros((1, 2 * p2), F32) for _ in range(bsz)))
    y = (jnp.dot(u, tt_ref[...], preferred_element_type=F32)
         + jnp.dot(sin_ref[:, 0:p2].astype(BF16), vt, preferred_element_type=F32))
    y_ref[...] = jnp.stack([y[:, co * q:(co + 1) * q] for co in range(ng)]).reshape(ng, bsz * nchunk * q)


def _s5_scan(u3, tabs, bsz):
    ptab, stab, ctab, bbrow, bbt, rot = tabs
    q = S5_CHUNK
    w = u3.shape[0] // 2
    n = u3.shape[1] // q
    g = w // S5_GROUP
    qc = S5_GROUP * q
    p2 = 2 * S5_STATE
    grp3 = lambda i: (i, 0, 0)
    grp4 = lambda i: (i, 0, 0, 0)
    return pl.pallas_call(
        functools.partial(_s5_scan_kernel, bsz=bsz, nchunk=n // bsz),
        grid=(g,),
        in_specs=[pl.BlockSpec((S5_GROUP, n * q), lambda i: (i, 0)),
                  pl.BlockSpec((1,) + ptab.shape[1:], grp4), pl.BlockSpec((1,) + stab.shape[1:], grp4),
                  pl.BlockSpec((1,) + ctab.shape[1:], grp3), pl.BlockSpec((1,) + bbrow.shape[1:], grp3),
                  pl.BlockSpec((1,) + bbt.shape[1:], grp3), pl.BlockSpec((1,) + rot.shape[1:], grp3)],
        out_specs=pl.BlockSpec((S5_GROUP, n * q), lambda i: (i, 0)),
        out_shape=jax.ShapeDtypeStruct((w, n * q), F32),
        scratch_shapes=[pltpu.VMEM((qc, qc), BF16), pltpu.VMEM((S5_GROUP, S5_GROUP, q), F32),
                        pltpu.VMEM((n, 2 * p2), F32), pltpu.VMEM((n, 2 * p2), F32)],
        compiler_params=pltpu.CompilerParams(dimension_semantics=("arbitrary",),
                                             vmem_limit_bytes=VMEM_LIMIT),
        name="s5_scan",
    )(u3, ptab, stab, ctab, bbrow, bbt, rot)


def _s5_gate_kernel(y_ref, u_ref, z_ref, d_ref, w_ref, b_ref, o_ref):
    y = y_ref[...] + d_ref[...] * u_ref[...]
    c = math.sqrt(2.0 / math.pi)
    g = 0.5 * y * (1.0 + jnp.tanh(c * (y + 0.044715 * (y * y * y))))
    gate = jax.nn.sigmoid(_bdot(w_ref[...], g) + b_ref[...])
    o_ref[...] = g * gate * _silu(z_ref[...])


def _s5_gate(yt, ht, d, wgt, b, tm):
    wd, t = yt.shape
    return pl.pallas_call(
        _s5_gate_kernel,
        grid=(t // tm,),
        in_specs=[pl.BlockSpec((wd, tm), lambda i: (0, i)),
                  pl.BlockSpec((wd, tm), lambda i: (0, i)),
                  pl.BlockSpec((wd, tm), lambda i: (1, i)),
                  _resident((wd, 1)), _resident((wd, wd)), _resident((wd, 1))],
        out_specs=pl.BlockSpec((wd, tm), lambda i: (0, i)),
        out_shape=jax.ShapeDtypeStruct((wd, t), F32),
        compiler_params=pltpu.CompilerParams(dimension_semantics=("arbitrary",),
                                             vmem_limit_bytes=VMEM_LIMIT),
        name="s5_gate",
    )(yt, ht, ht, d, wgt, b)


def _s5_tables(lam_re, lam_im, log_step, b_re, b_im, c_re, c_im):
    q = S5_CHUNK
    step = jnp.exp(log_step)[..., None]
    zr, zi = lam_re * step, lam_im * step
    er = jnp.exp(zr)
    nr, ni = er * jnp.cos(zi) - 1.0, er * jnp.sin(zi)
    den = lam_re * lam_re + lam_im * lam_im
    fr = (nr * lam_re + ni * lam_im) / den
    fi = (ni * lam_re - nr * lam_im) / den
    bbr = fr[..., None] * b_re - fi[..., None] * b_im
    bbi = fr[..., None] * b_im + fi[..., None] * b_re

    def powers(e, zre, zim):
        m = jnp.exp(e * zre)
        return m * jnp.cos(e * zim), m * jnp.sin(e * zim)

    lag = jnp.arange(q, dtype=F32)
    pr0, pi0 = powers(lag, zr[..., None], zi[..., None])
    pr1, pi1 = powers(lag + 1.0, zr[..., None], zi[..., None])
    cat = lambda a, b: jnp.concatenate([a, b], axis=2)
    ptab = jnp.stack([cat(pr0, -pi0), cat(-pi0, -pr0), cat(pr1, -pi1), cat(-pi1, -pr1)], axis=2)
    rs, ims = powers((q - 1.0 - lag)[:, None], zr[:, :, None, :], zi[:, :, None, :])
    stab = jnp.stack([jnp.concatenate([rs] * 4, -1), jnp.concatenate([-ims, ims, ims, -ims], -1)], axis=2)
    tr = lambda a: jnp.swapaxes(a, 2, 3)
    ctab = jnp.concatenate([tr(jnp.concatenate([c_re, c_re], -1)), tr(jnp.concatenate([c_im, c_im], -1))], -1)
    bbrow = jnp.concatenate([tr(jnp.concatenate([bbr, bbi, bbi, bbr], 2)),
                             tr(jnp.concatenate([bbi, bbr, bbr, bbi], 2))], 2)
    bbt = tr(jnp.concatenate([bbr, bbi], 2))
    prq, piq = powers(float(q), zr, zi)
    rot = jnp.stack([jnp.concatenate([prq] * 4, -1),
                     jnp.concatenate([-piq, piq, piq, -piq], -1)], axis=2)
    return ptab, stab, ctab, bbrow, bbt, rot


def _s5_mixer(ht, tabs, d, wgt, bg, bsz, tm):
    wd = ht.shape[0] // 2
    t = ht.shape[1]
    yt = _s5_scan(ht, tabs, bsz)
    return _s5_gate(yt, ht, d, wgt, bg, tm)


def _mlstm_kernel(h_ref, gb_ref, ng_ref, o_ref, c_ref, m_ref, *, group):
    q = ML_CHUNK
    nqk = 2 * ML_HEADS * ML_DQK_PAD
    wv = ML_HEADS * ML_DV
    bsz = h_ref.shape[0]

    @pl.when(pl.program_id(0) == 0)
    def _():
        c_ref[...] = jnp.zeros(c_ref.shape, F32)
        m_ref[...] = jnp.zeros(m_ref.shape, F32)

    row = lax.broadcasted_iota(jnp.int32, (q, q), 0)
    col = lax.broadcasted_iota(jnp.int32, (q, q), 1)
    causal = col <= row
    tril = causal.astype(F32)
    lane = lax.broadcasted_iota(jnp.int32, (1, WIN), 1)
    one_b, zero_b = jnp.ones((), BF16), jnp.zeros((), BF16)

    gates, bc, gates_t, bc_t = [], [], [], []
    for b in range(bsz):
        g = h_ref[b, :, nqk + 3 * wv:nqk + 3 * wv + LANES] + gb_ref[...]
        c = jnp.dot(tril, -_softplus(-g), precision=HI, preferred_element_type=F32)
        gates.append(g)
        bc.append(c)
        gates_t.append(g.T)
        bc_t.append(c.T)

    hn = {}
    for par in (0, 1):
        allp = [(b, hd) for hd in range(par, ML_HEADS, 2) for b in range(bsz)]
        first_valid = _window(par)[1]
        vmask = (lane >= first_valid) if first_valid else (lane < WIN_VALID)
        nlane = 0 if first_valid else WIN_VALID
        for g0 in range(0, len(allp), group):
            pairs = allp[g0:g0 + group]
            idx = [b * ML_HEADS + hd for b, hd in pairs]
            n = len(pairs)

            def stack(fn):
                return jnp.stack([fn(b, hd) for b, hd in pairs])

            vext = [jnp.where(lane == nlane, one_b,
                              jnp.where(vmask, h_ref[b, :, nqk + _window(hd)[0]:nqk + _window(hd)[0] + WIN].astype(BF16),
                                        zero_b)) for b, hd in pairs]
            qh = [(h_ref[b, :, hd * ML_DQK_PAD:(hd + 1) * ML_DQK_PAD] * (ML_DQK ** -0.5)).astype(BF16)
                  for b, hd in pairs]
            kh = stack(lambda b, hd: h_ref[b, :, (ML_HEADS + hd) * ML_DQK_PAD:(ML_HEADS + hd + 1) * ML_DQK_PAD])
            bcol = stack(lambda b, hd: bc[b][:, ML_HEADS + hd:ML_HEADS + hd + 1])
            brow = stack(lambda b, hd: bc_t[b][ML_HEADS + hd:ML_HEADS + hd + 1, :])
            icol = stack(lambda b, hd: gates[b][:, hd:hd + 1])
            irow = stack(lambda b, hd: gates_t[b][hd:hd + 1, :])
            m_st = jnp.stack([m_ref[i, 0:1, 0:1] for i in idx])
            c_st = [c_ref[i] for i in idx]

            log_d = jnp.where(causal, bcol + (irow - brow), -jnp.inf)
            inter = bcol + m_st
            m_row = jnp.maximum(inter, jnp.max(log_d, -1, keepdims=True))
            dmat = jnp.exp(log_d - m_row)
            isc = jnp.exp(inter - m_row)
            s = (jnp.stack([_bdot_t1(qh[i], kh[i]) for i in range(n)]) * dmat).astype(BF16)
            num = (jnp.stack([jnp.dot(s[i], vext[i], preferred_element_type=F32) for i in range(n)])
                   + isc * jnp.stack([_bdot(qh[i], c_st[i]) for i in range(n)]))
            den = num[:, :, nlane:nlane + 1]
            hh = num * (1.0 / jnp.maximum(jnp.abs(den), jnp.exp(-m_row)))
            hv = jnp.where(lane == nlane, 0.0, hh)
            ms = jnp.sum(hv * hv, -1, keepdims=True) * (1.0 / ML_DV)
            hnorm = hv * lax.rsqrt(ms + EPS)

            blast = bcol[:, q - 1:q, :]
            log_w = blast - bcol + icol
            m_new = jnp.maximum(blast + m_st, jnp.max(log_w, 1, keepdims=True))
            kw = (kh * jnp.exp(log_w - m_new)).astype(BF16)
            decay = jnp.exp(blast + m_st - m_new)
            for i in range(n):
                c_ref[idx[i]] = decay[i] * c_st[i] + _bdot_t0(kw[i], vext[i])
                m_ref[idx[i]] = jnp.broadcast_to(m_new[i], m_ref.shape[1:])
                hn[pairs[i]] = hnorm[i]

    for b in range(bsz):
        joined = _join_windows([hn[(b, hd)] for hd in range(ML_HEADS)])
        o_sig = h_ref[b, :, nqk + wv:nqk + 2 * wv]
        z_act = h_ref[b, :, nqk + 2 * wv:nqk + 3 * wv]
        o_ref[b] = o_sig * (joined * ng_ref[...]) * z_act


def _mlstm_mixer(h3, gb, ng):
    bsz, seq, n = h3.shape
    wv = ML_HEADS * ML_DV
    blk = lambda c: (0, c, 0)
    return pl.pallas_call(
        functools.partial(_mlstm_kernel, group=ML_GROUP),
        grid=(seq // ML_CHUNK,),
        in_specs=[pl.BlockSpec((bsz, ML_CHUNK, n), blk), _resident((1, LANES)), _resident((1, wv))],
        out_specs=pl.BlockSpec((bsz, ML_CHUNK, wv), blk),
        out_shape=jax.ShapeDtypeStruct((bsz, seq, wv), F32),
        scratch_shapes=[pltpu.VMEM((bsz * ML_HEADS, ML_DQK_PAD, WIN), F32),
                        pltpu.VMEM((bsz * ML_HEADS, TAIL, LANES), F32)],
        compiler_params=pltpu.CompilerParams(dimension_semantics=("arbitrary",),
                                             vmem_limit_bytes=VMEM_LIMIT),
        name="mlstm",
    )(h3, gb, ng)


def _ssd_kernel(h_ref, dtb_ref, alog_ref, dsk_ref, ng_ref, o_ref, s_ref):
    q = SSD_CHUNK
    wx = SSD_HEADS * SSD_HEADDIM
    wbc = SSD_GROUPS * SSD_STATE
    nconv = wx + 2 * wbc
    bsz = h_ref.shape[0]

    @pl.when(pl.program_id(0) == 0)
    def _():
        s_ref[...] = jnp.zeros(s_ref.shape, F32)

    a = -jnp.exp(alog_ref[...])
    row = lax.broadcasted_iota(jnp.int32, (q, q), 0)
    col = lax.broadcasted_iota(jnp.int32, (q, q), 1)
    causal = col <= row
    tril = causal.astype(F32)
    lane = lax.broadcasted_iota(jnp.int32, (1, WIN), 1)
    lo_half = lane[:, :LANES] < SSD_HEADDIM
    zero_b = jnp.zeros((), BF16)

    def per_channel(hcols):
        return jnp.concatenate([jnp.where(lo_half, hcols[..., 2 * p:2 * p + 1], hcols[..., 2 * p + 1:2 * p + 2])
                                for p in range(SSD_HEADS // 2)], axis=-1)

    x = h_ref[:, :, 0:wx]
    dt = _softplus(h_ref[:, :, nconv + wx:nconv + wx + LANES] + dtb_ref[...])
    dta = dt * a
    acum = jnp.stack([jnp.dot(tril, dta[b], precision=HI, preferred_element_type=F32)
                      for b in range(bsz)])
    acum_t = [acum[b].T for b in range(bsz)]
    alast = acum[:, q - 1:q, :]
    eac_c = per_channel(jnp.exp(acum))
    dtx = x * per_channel(dt)
    dtx_dec = dtx * per_channel(jnp.exp(alast - acum))

    ywins = [[] for _ in range(bsz)]
    for g in range(SSD_GROUPS):
        w0, first_valid = _window(g)
        win = slice(w0, w0 + WIN)
        gmask = (lane >= first_valid) if first_valid else (lane < WIN_VALID)
        bm = [h_ref[b, :, wx + g * SSD_STATE:wx + (g + 1) * SSD_STATE].astype(BF16) for b in range(bsz)]
        cm = [h_ref[b, :, wx + wbc + g * SSD_STATE:wx + wbc + (g + 1) * SSD_STATE].astype(BF16) for b in range(bsz)]
        cbm = jnp.stack([_bdot_t1(cm[b], bm[b]) for b in range(bsz)])
        s_prev = [s_ref[b * SSD_GROUPS + g] for b in range(bsz)]
        xw = dtx[:, :, win].astype(BF16)
        xdec = jnp.where(gmask, dtx_dec[:, :, win].astype(BF16), zero_b)
        heads = [g * SSD_RATIO + k for k in range(SSD_RATIO)]
        acol = jnp.concatenate([acum[:, :, hd:hd + 1] for hd in heads], axis=0)
        arow = jnp.stack([acum_t[b][hd:hd + 1, :] for hd in heads for b in range(bsz)])
        lmat = jnp.exp(jnp.where(causal, acol - arow, -jnp.inf))
        mmat = (jnp.concatenate([cbm] * SSD_RATIO, axis=0) * lmat).astype(BF16)
        yoff = jnp.stack([_bdot(cm[b], s_prev[b]) for b in range(bsz)]) * eac_c[:, :, win]
        for b in range(bsz):
            y = yoff[b]
            for k in range(SSD_RATIO):
                lo = first_valid + k * SSD_HEADDIM
                hmask = (lane >= lo) & (lane < lo + SSD_HEADDIM)
                y = y + jnp.dot(mmat[k * bsz + b], jnp.where(hmask, xw[b], zero_b), preferred_element_type=F32)
            ywins[b].append(y)
            s_ref[b * SSD_GROUPS + g] = s_prev[b] * eac_c[b, q - 1:q, win] + _bdot_t0(bm[b], xdec[b])

    for b in range(bsz):
        yz = (_join_windows(ywins[b]) + dsk_ref[...] * x[b]) * h_ref[b, :, nconv:nconv + wx]
        ms = jnp.mean(yz * yz, -1, keepdims=True)
        o_ref[b] = yz * lax.rsqrt(ms + EPS) * ng_ref[...]


def _ssd_mixer(h3, dtb, alog, dsk, ng):
    bsz, seq, n = h3.shape
    wx = SSD_HEADS * SSD_HEADDIM
    blk = lambda c: (0, c, 0)
    return pl.pallas_call(
        _ssd_kernel,
        grid=(seq // SSD_CHUNK,),
        in_specs=[pl.BlockSpec((bsz, SSD_CHUNK, n), blk),
                  _resident((1, LANES)), _resident((1, LANES)),
                  _resident((1, wx)), _resident((1, wx))],
        out_specs=pl.BlockSpec((bsz, SSD_CHUNK, wx), blk),
        out_shape=jax.ShapeDtypeStruct((bsz, seq, wx), F32),
        scratch_shapes=[pltpu.VMEM((bsz * SSD_GROUPS, SSD_STATE, WIN), F32)],
        compiler_params=pltpu.CompilerParams(dimension_semantics=("arbitrary",),
                                             vmem_limit_bytes=VMEM_LIMIT),
        name="ssd",
    )(h3, dtb, alog, dsk, ng)


def _pad_last(a, n):
    return jnp.pad(a, [(0, 0)] * (a.ndim - 1) + [(0, n - a.shape[-1])])


def _pad_heads(a, heads, width, padded):
    s = a.shape[:-1]
    a = a.reshape(s + (heads, width))
    return _pad_last(a, padded).reshape(s + (heads * padded,))


def kernel(x, w_in, w_out, ln_g, ln_b, s5_lambda_re, s5_lambda_im, s5_log_step, s5_b_re, s5_b_im, s5_c_re, s5_c_im, s5_d, s5_w_glu, s5_b_glu, ml_conv_w, ml_conv_b, ml_i_bias, ml_f_bias, ml_norm_g, ssd_conv_w, ssd_conv_b, ssd_dt_bias, ssd_a_log, ssd_d, ssd_norm_g):
    bsz, seq, dm = x.shape
    depth = w_in.shape[0]
    t = bsz * seq
    s5w = s5_d.shape[-1]
    mlw = ML_HEADS * ML_DV
    mlqk = ML_HEADS * ML_DQK
    ssdw = SSD_HEADS * SSD_HEADDIM
    ssdbc = SSD_GROUPS * SSD_STATE
    alpha = (2.0 * depth) ** 0.25
    tm = min(256, seq)

    sizes = (s5w, s5w, 2 * mlqk, mlw, ML_HEADS, ML_HEADS, mlw, mlw, ssdw + 2 * ssdbc, SSD_HEADS, ssdw)
    offs = [0]
    for sz in sizes:
        offs.append(offs[-1] + sz)
    col = lambda i: w_in[:, :, offs[i]:offs[i + 1]]
    w_s5t = jnp.swapaxes(w_in[:, :, :2 * s5w], 1, 2).astype(BF16)
    qkw = col(2)
    w_ml = jnp.concatenate([
        _pad_heads(qkw[..., :mlqk], ML_HEADS, ML_DQK, ML_DQK_PAD),
        _pad_heads(qkw[..., mlqk:], ML_HEADS, ML_DQK, ML_DQK_PAD),
        col(3), col(6), col(7),
        _pad_last(jnp.concatenate([col(4), col(5)], -1), LANES)], -1).astype(BF16)
    w_ssd = jnp.concatenate([col(8), col(10), _pad_last(col(9), LANES)], -1).astype(BF16)

    ml_cw = jnp.concatenate([_pad_heads(ml_conv_w[..., :mlqk], ML_HEADS, ML_DQK, ML_DQK_PAD),
                             _pad_heads(ml_conv_w[..., mlqk:], ML_HEADS, ML_DQK, ML_DQK_PAD)], -1)
    ml_cb = jnp.concatenate([_pad_heads(ml_conv_b[..., :mlqk], ML_HEADS, ML_DQK, ML_DQK_PAD),
                             _pad_heads(ml_conv_b[..., mlqk:], ML_HEADS, ML_DQK, ML_DQK_PAD)], -1)
    ml_gb = _pad_last(jnp.concatenate([ml_i_bias, ml_f_bias], -1), LANES)
    ssd_dtb = _pad_last(ssd_dt_bias, LANES)
    ssd_al = _pad_last(ssd_a_log, LANES)
    ssd_dsk = jnp.repeat(ssd_d, SSD_HEADDIM, axis=-1)
    w_o = w_out.astype(BF16)
    s5_wgt = jnp.swapaxes(s5_w_glu, 1, 2).astype(BF16)
    s5_tabs = _s5_tables(s5_lambda_re, s5_lambda_im, s5_log_step, s5_b_re, s5_b_im, s5_c_re, s5_c_im)

    x2 = x.reshape(t, dm)
    for l in range(depth):
        ht_s5, h_ml, h_ssd = _inproj(x2, w_s5t[l], w_ml[l], w_ssd[l], ml_cw[l], ml_cb[l][None],
                                     ssd_conv_w[l], ssd_conv_b[l][None], tm, seq)
        y_s5 = _s5_mixer(ht_s5, tuple(tb[l] for tb in s5_tabs), s5_d[l][:, None], s5_wgt[l],
                         s5_b_glu[l][:, None], bsz, tm)
        y_ml = _mlstm_mixer(h_ml.reshape(bsz, seq, -1), ml_gb[l][None], ml_norm_g[l][None])
        y_ssd = _ssd_mixer(h_ssd.reshape(bsz, seq, -1), ssd_dtb[l][None], ssd_al[l][None],
                           ssd_dsk[l][None], ssd_norm_g[l][None])
        x2 = _outproj(x2, y_s5, y_ml.reshape(t, mlw), y_ssd.reshape(t, ssdw),
                      w_o[l, :s5w], w_o[l, s5w:s5w + mlw], w_o[l, s5w + mlw:],
                      ln_g[l][None], ln_b[l][None], tm, alpha)
    return x2.reshape(bsz, seq, dm)
```

```python
import functools
import math

import jax
import jax.numpy as jnp
from jax import lax
from jax.experimental import pallas as pl
from jax.experimental.pallas import tpu as pltpu

F32 = jnp.float32
BF16 = jnp.bfloat16
HI = lax.Precision.HIGHEST

LANES = 128
TAIL = 8
VMEM_LIMIT = 56 * 1024 * 1024

CONV_K = 4
OUT_TILE = 512
OUT_SUB = 256
EPS = 1e-5

S5_GROUP = 16
S5_STATE = 64
S5_CHUNK = LANES

ML_HEADS = 4
ML_DV = 192
ML_DQK = 96
ML_DQK_PAD = 128
ML_CHUNK = 128
ML_GROUP = 8

SSD_HEADDIM = 64
SSD_HEADS = 12
SSD_GROUPS = 4
SSD_RATIO = SSD_HEADS // SSD_GROUPS
SSD_STATE = 128
SSD_CHUNK = 128

WIN = 256
WIN_VALID = 192
assert ML_DV == WIN_VALID and SSD_RATIO * SSD_HEADDIM == WIN_VALID


def _silu(x):
    return x * jax.nn.sigmoid(x)


def _softplus(x):
    return jnp.maximum(x, 0.0) + jnp.log1p(jnp.exp(-jnp.abs(x)))


def _bdot(a, b):
    return jnp.dot(a.astype(BF16), b.astype(BF16), preferred_element_type=F32)


def _bdot_t0(a, b):
    return lax.dot_general(a.astype(BF16), b.astype(BF16), (((0,), (0,)), ((), ())),
                           preferred_element_type=F32)


def _bdot_t1(a, b):
    return lax.dot_general(a.astype(BF16), b.astype(BF16), (((1,), (1,)), ((), ())),
                           preferred_element_type=F32)


def _resident(shape):
    zeros = (0,) * len(shape)
    return pl.BlockSpec(shape, lambda *_: zeros, pipeline_mode=pl.Buffered(1))


def _window(idx):
    start = (idx // 2) * 2 * WIN_VALID + (LANES if idx % 2 else 0)
    return start, (WIN - WIN_VALID if idx % 2 else 0)


def _join_windows(wins):
    tiles = []
    for pr in range(len(wins) // 2):
        ev, od = wins[2 * pr], wins[2 * pr + 1]
        tiles += [ev[:, :LANES], ev[:, LANES:] + od[:, :LANES], od[:, LANES:]]
    return jnp.concatenate(tiles, axis=1)


def _inproj_kernel(x_ref, ws5_ref, wml_ref, wssd_ref, mcw_ref, mcb_ref, scw_ref, scb_ref,
                   os5_ref, oml_ref, ossd_ref, mtail_ref, stail_ref, mbuf_ref, sbuf_ref,
                   *, tn, tiles_per_seq, ml_segs, ssd_segs):
    tm = x_ref.shape[0]

    @pl.when(pl.program_id(0) % tiles_per_seq == 0)
    def _():
        mtail_ref[...] = jnp.zeros(mtail_ref.shape, F32)
        stail_ref[...] = jnp.zeros(stail_ref.shape, F32)

    xb = x_ref[...].astype(BF16)
    hs5 = lax.dot_general(ws5_ref[...], xb, (((1,), (1,)), ((), ())), preferred_element_type=F32)
    half = os5_ref.shape[0] // 2
    os5_ref[0:half, :] = hs5[0:half, :]
    os5_ref[half:, :] = _silu(hs5[half:, :])

    def conv_silu(w_ref, o_ref, cw_ref, cb_ref, tail_ref, buf_ref, width):
        for j in range(0, width, tn):
            e = min(j + tn, width)
            r = jnp.dot(xb, w_ref[:, j:e], preferred_element_type=F32)
            buf_ref[0:TAIL, j:e] = tail_ref[:, j:e]
            buf_ref[TAIL:TAIL + tm, j:e] = r
            acc = cb_ref[:, j:e] + cw_ref[CONV_K - 1:CONV_K, j:e] * r
            for k in range(CONV_K - 1):
                off = TAIL - (CONV_K - 1) + k
                acc = acc + cw_ref[k:k + 1, j:e] * buf_ref[off:off + tm, j:e]
            tail_ref[:, j:e] = r[tm - TAIL:tm, :]
            o_ref[:, j:e] = _silu(acc)

    def plain(w_ref, o_ref, lo, hi, fn):
        for j in range(lo, hi, tn):
            e = min(j + tn, hi)
            r = jnp.dot(xb, w_ref[:, j:e], preferred_element_type=F32)
            o_ref[:, j:e] = r if fn is None else fn(r)

    conv_silu(wml_ref, oml_ref, mcw_ref, mcb_ref, mtail_ref, mbuf_ref, ml_segs[0][0])
    for lo, hi, fn in ml_segs[1:]:
        plain(wml_ref, oml_ref, lo, hi, fn)
    conv_silu(wssd_ref, ossd_ref, scw_ref, scb_ref, stail_ref, sbuf_ref, ssd_segs[0][0])
    for lo, hi, fn in ssd_segs[1:]:
        plain(wssd_ref, ossd_ref, lo, hi, fn)


def _inproj(x2, ws5t, wml, wssd, mcw, mcb, scw, scb, tm, seq):
    t, k = x2.shape
    ns5, nml, nssd = ws5t.shape[0], wml.shape[1], wssd.shape[1]
    nqk, nconv = mcw.shape[1], scw.shape[1]
    wv = ML_HEADS * ML_DV
    wx = SSD_HEADS * SSD_HEADDIM
    ml_segs = ((nqk,), (nqk, nqk + wv, None), (nqk + wv, nqk + 2 * wv, jax.nn.sigmoid),
               (nqk + 2 * wv, nqk + 3 * wv, _silu), (nqk + 3 * wv, nml, None))
    ssd_segs = ((nconv,), (nconv, nconv + wx, _silu), (nconv + wx, nssd, None))
    return pl.pallas_call(
        functools.partial(_inproj_kernel, tn=512, tiles_per_seq=seq // tm, ml_segs=ml_segs, ssd_segs=ssd_segs),
        grid=(t // tm,),
        in_specs=[pl.BlockSpec((tm, k), lambda i: (i, 0)),
                  _resident(ws5t.shape), _resident(wml.shape), _resident(wssd.shape),
                  _resident(mcw.shape), _resident(mcb.shape), _resident(scw.shape), _resident(scb.shape)],
        out_specs=[pl.BlockSpec((ns5, tm), lambda i: (0, i)),
                   pl.BlockSpec((tm, nml), lambda i: (i, 0)),
                   pl.BlockSpec((tm, nssd), lambda i: (i, 0))],
        out_shape=[jax.ShapeDtypeStruct((ns5, t), F32),
                   jax.ShapeDtypeStruct((t, nml), F32),
                   jax.ShapeDtypeStruct((t, nssd), F32)],
        scratch_shapes=[pltpu.VMEM((TAIL, nqk), F32), pltpu.VMEM((TAIL, nconv), F32),
                        pltpu.VMEM((TAIL + tm, nqk), F32), pltpu.VMEM((TAIL + tm, nconv), F32)],
        compiler_params=pltpu.CompilerParams(dimension_semantics=("arbitrary",),
                                             vmem_limit_bytes=VMEM_LIMIT),
        name="inproj",
    )(x2, ws5t, wml, wssd, mcw, mcb, scw, scb)


def _outproj_kernel(x_ref, y_ref, u_ref, z_ref, b_ref, c_ref, d_ref, wg_ref, bg_ref,
                    wa_ref, wb_ref, wc_ref, g_ref, bb_ref, o_ref, *, alpha, sub):
    c0 = math.sqrt(2.0 / math.pi)
    for r0 in range(0, x_ref.shape[0], sub):
        rows = slice(r0, r0 + sub)
        ys = y_ref[:, rows] + d_ref[...] * u_ref[:, rows]
        gl = 0.5 * ys * (1.0 + jnp.tanh(c0 * (ys + 0.044715 * (ys * ys * ys))))
        a = gl * jax.nn.sigmoid(_bdot(wg_ref[...], gl) + bg_ref[...]) * z_ref[:, rows]
        out = (_bdot_t0(a, wa_ref[...]) + _bdot(b_ref[rows, :], wb_ref[...]) + _bdot(c_ref[rows, :], wc_ref[...]))
        y = alpha * x_ref[rows, :] + out
        mu = jnp.mean(y, -1, keepdims=True)
        d = y - mu
        var = jnp.mean(d * d, -1, keepdims=True)
        o_ref[rows, :] = d * lax.rsqrt(var + EPS) * g_ref[...] + bb_ref[...]


def _outproj(x2, yt, ht, mb, mc, d5, wgt, bg5, wa, wb, wc, g, b, tm, alpha):
    t, d = x2.shape
    wd = yt.shape[0]
    row = lambda i: (i, 0)
    return pl.pallas_call(
        functools.partial(_outproj_kernel, alpha=alpha, sub=min(OUT_SUB, tm)),
        grid=(t // tm,),
        in_specs=[pl.BlockSpec((tm, d), row),
                  pl.BlockSpec((wd, tm), lambda i: (0, i)),
                  pl.BlockSpec((wd, tm), lambda i: (0, i)),
                  pl.BlockSpec((wd, tm), lambda i: (1, i)),
                  pl.BlockSpec((tm, mb.shape[1]), row),
                  pl.BlockSpec((tm, mc.shape[1]), row),
                  _resident((wd, 1)), _resident((wd, wd)), _resident((wd, 1)),
                  _resident(wa.shape), _resident(wb.shape), _resident(wc.shape),
                  _resident((1, d)), _resident((1, d))],
        out_specs=pl.BlockSpec((tm, d), row),
        out_shape=jax.ShapeDtypeStruct((t, d), F32),
        compiler_params=pltpu.CompilerParams(dimension_semantics=("arbitrary",),
                                             vmem_limit_bytes=VMEM_LIMIT),
        name="outproj_ln",
    )(x2, yt, ht, ht, mb, mc, d5, wgt, bg5, wa, wb, wc, g, b)


def _s5_scan_kernel(u_ref, ptab_ref, stab_ref, ctab_ref, bbrow_ref, bbt_ref, rot_ref, y_ref,
                    tta_ref, ttb_ref, kk_ref, up_ref, acc_ref, sloc_ref, sin_ref, *, bsz, nchunk):
    q = S5_CHUNK
    p2 = 2 * S5_STATE
    ng = S5_GROUP
    srow = lax.broadcasted_iota(jnp.int32, (q, q), 0)
    tcol = lax.broadcasted_iota(jnp.int32, (q, q), 1)
    upper = tcol >= srow

    ctab = ctab_ref[0]

    def c_lam(co, a, b):
        return ctab[:, co:co + 1] * a + ctab[:, ng + co:ng + co + 1] * b

    cp0 = jnp.concatenate([c_lam(co, ptab_ref[0, 0], ptab_ref[0, 1]) for co in range(ng)], axis=1)
    vt = jnp.concatenate([c_lam(co, ptab_ref[0, 2], ptab_ref[0, 3]) for co in range(ng)], axis=1).astype(BF16)
    kk = jnp.dot(bbt_ref[0], cp0, precision=HI, preferred_element_type=F32)
    for co in range(ng):
        kk_ref[co] = kk[:, co * q:(co + 1) * q]

    u3 = u_ref[...].reshape(ng, bsz * nchunk, q)
    for pr in range(ng // 2):
        up_ref[pr] = jnp.concatenate([u3[2 * pr], u3[2 * pr + 1]], axis=1).astype(BF16)
    rs, ims = stab_ref[0, 0], stab_ref[0, 1]
    sloc = None
    for pr in range(ng // 2):
        wt = jnp.concatenate(
            [(rs * bbrow_ref[0, ci:ci + 1, :] + ims * bbrow_ref[0, ng + ci:ng + ci + 1, :]).astype(BF16)
             for ci in (2 * pr, 2 * pr + 1)], axis=0)
        part = jnp.dot(up_ref[pr], wt, preferred_element_type=F32)
        sloc = part if sloc is None else sloc + part
    sloc_ref[...] = sloc
    a1 = rot_ref[0, 0:1, :]
    a2 = rot_ref[0, 1:2, :]

    def body(c, states):
        nxt = []
        for b in range(bsz):
            r = pl.ds(b * nchunk + c, 1)
            s = states[b]
            sin_ref[r, :] = s
            swapped = jnp.concatenate([s[:, p2:], s[:, :p2]], axis=1)
            nxt.append(a1 * s + a2 * swapped + sloc_ref[r, :])
        return tuple(nxt)

    lax.fori_loop(0, nchunk, body, tuple(jnp.zeros((1, 2 * p2), F32) for _ in range(bsz)))
    acc_ref[...] = jnp.dot(sin_ref[:, 0:p2].astype(BF16), vt, preferred_element_type=F32)

    def build_pair(pr, t_ref):
        for k in range(2):
            for co in range(ng):
                lag = kk_ref[co, pl.ds(2 * pr + k, 1), :]
                blk = pltpu.roll(jnp.broadcast_to(lag, (q, q)), 0, 1, stride=1, stride_axis=0)
                t_ref[k * q:(k + 1) * q, co * q:(co + 1) * q] = jnp.where(upper, blk, 0.0).astype(BF16)

    def dot_pair(pr, t_ref):
        acc_ref[...] += jnp.dot(up_ref[pr], t_ref[...], preferred_element_type=F32)

    npair = ng // 2
    build_pair(0, tta_ref)

    def step(j, carry):
        build_pair(2 * j + 1, ttb_ref)
        dot_pair(2 * j, tta_ref)
        build_pair(2 * j + 2, tta_ref)
        dot_pair(2 * j + 1, ttb_ref)
        return carry

    lax.fori_loop(0, npair // 2 - 1, step, 0)
    build_pair(npair - 1, ttb_ref)
    dot_pair(npair - 2, tta_ref)
    dot_pair(npair - 1, ttb_ref)
    y = acc_ref[...]
    y_ref[...] = jnp.stack([y[:, co * q:(co + 1) * q] for co in range(ng)]).reshape(ng, bsz * nchunk * q)


def _s5_scan(u3, tabs, bsz):
    ptab, stab, ctab, bbrow, bbt, rot = tabs
    q = S5_CHUNK
    w = u3.shape[0] // 2
    n = u3.shape[1] // q
    g = w // S5_GROUP
    qc = S5_GROUP * q
    p2 = 2 * S5_STATE
    grp3 = lambda i: (i, 0, 0)
    grp4 = lambda i: (i, 0, 0, 0)
    return pl.pallas_call(
        functools.partial(_s5_scan_kernel, bsz=bsz, nchunk=n // bsz),
        grid=(g,),
        in_specs=[pl.BlockSpec((S5_GROUP, n * q), lambda i: (i, 0)),
                  pl.BlockSpec((1,) + ptab.shape[1:], grp4), pl.BlockSpec((1,) + stab.shape[1:], grp4),
                  pl.BlockSpec((1,) + ctab.shape[1:], grp3), pl.BlockSpec((1,) + bbrow.shape[1:], grp3),
                  pl.BlockSpec((1,) + bbt.shape[1:], grp3), pl.BlockSpec((1,) + rot.shape[1:], grp3)],
        out_specs=pl.BlockSpec((S5_GROUP, n * q), lambda i: (i, 0)),
        out_shape=jax.ShapeDtypeStruct((w, n * q), F32),
        scratch_shapes=[pltpu.VMEM((2 * q, qc), BF16), pltpu.VMEM((2 * q, qc), BF16),
                        pltpu.VMEM((S5_GROUP, S5_GROUP, q), F32),
                        pltpu.VMEM((S5_GROUP // 2, n, 2 * q), BF16), pltpu.VMEM((n, qc), F32),
                        pltpu.VMEM((n, 2 * p2), F32), pltpu.VMEM((n, 2 * p2), F32)],
        compiler_params=pltpu.CompilerParams(dimension_semantics=("arbitrary",),
                                             vmem_limit_bytes=VMEM_LIMIT),
        name="s5_scan",
    )(u3, ptab, stab, ctab, bbrow, bbt, rot)


def _s5_tables(lam_re, lam_im, log_step, b_re, b_im, c_re, c_im):
    q = S5_CHUNK
    step = jnp.exp(log_step)[..., None]
    zr, zi = lam_re * step, lam_im * step
    er = jnp.exp(zr)
    nr, ni = er * jnp.cos(zi) - 1.0, er * jnp.sin(zi)
    den = lam_re * lam_re + lam_im * lam_im
    fr = (nr * lam_re + ni * lam_im) / den
    fi = (ni * lam_re - nr * lam_im) / den
    bbr = fr[..., None] * b_re - fi[..., None] * b_im
    bbi = fr[..., None] * b_im + fi[..., None] * b_re

    def powers(e, zre, zim):
        m = jnp.exp(e * zre)
        return m * jnp.cos(e * zim), m * jnp.sin(e * zim)

    lag = jnp.arange(q, dtype=F32)
    pr0, pi0 = powers(lag, zr[..., None], zi[..., None])
    pr1, pi1 = powers(lag + 1.0, zr[..., None], zi[..., None])
    cat = lambda a, b: jnp.concatenate([a, b], axis=2)
    ptab = jnp.stack([cat(pr0, -pi0), cat(-pi0, -pr0), cat(pr1, -pi1), cat(-pi1, -pr1)], axis=2)
    rs, ims = powers((q - 1.0 - lag)[:, None], zr[:, :, None, :], zi[:, :, None, :])
    stab = jnp.stack([jnp.concatenate([rs] * 4, -1), jnp.concatenate([-ims, ims, ims, -ims], -1)], axis=2)
    tr = lambda a: jnp.swapaxes(a, 2, 3)
    ctab = jnp.concatenate([tr(jnp.concatenate([c_re, c_re], -1)), tr(jnp.concatenate([c_im, c_im], -1))], -1)
    bbrow = jnp.concatenate([tr(jnp.concatenate([bbr, bbi, bbi, bbr], 2)),
                             tr(jnp.concatenate([bbi, bbr, bbr, bbi], 2))], 2)
    bbt = tr(jnp.concatenate([bbr, bbi], 2))
    prq, piq = powers(float(q), zr, zi)
    rot = jnp.stack([jnp.concatenate([prq] * 4, -1),
                     jnp.concatenate([-piq, piq, piq, -piq], -1)], axis=2)
    return ptab, stab, ctab, bbrow, bbt, rot


def _mlstm_kernel(h_ref, gb_ref, ng_ref, o_ref, c_ref, m_ref, *, group):
    q = ML_CHUNK
    nqk = 2 * ML_HEADS * ML_DQK_PAD
    wv = ML_HEADS * ML_DV
    bsz = h_ref.shape[0]

    @pl.when(pl.program_id(0) == 0)
    def _():
        c_ref[...] = jnp.zeros(c_ref.shape, F32)
        m_ref[...] = jnp.zeros(m_ref.shape, F32)

    row = lax.broadcasted_iota(jnp.int32, (q, q), 0)
    col = lax.broadcasted_iota(jnp.int32, (q, q), 1)
    causal = col <= row
    tril = causal.astype(F32)
    lane = lax.broadcasted_iota(jnp.int32, (1, WIN), 1)
    one_b, zero_b = jnp.ones((), BF16), jnp.zeros((), BF16)

    gates, bc, gates_t, bc_t = [], [], [], []
    for b in range(bsz):
        g = h_ref[b, :, nqk + 3 * wv:nqk + 3 * wv + LANES] + gb_ref[...]
        c = jnp.dot(tril, -_softplus(-g), precision=HI, preferred_element_type=F32)
        gates.append(g)
        bc.append(c)
        gates_t.append(g.T)
        bc_t.append(c.T)

    hn = {}
    for par in (0, 1):
        allp = [(b, hd) for hd in range(par, ML_HEADS, 2) for b in range(bsz)]
        first_valid = _window(par)[1]
        vmask = (lane >= first_valid) if first_valid else (lane < WIN_VALID)
        nlane = 0 if first_valid else WIN_VALID
        for g0 in range(0, len(allp), group):
            pairs = allp[g0:g0 + group]
            idx = [b * ML_HEADS + hd for b, hd in pairs]
            n = len(pairs)

            def stack(fn):
                return jnp.stack([fn(b, hd) for b, hd in pairs])

            vext = [jnp.where(lane == nlane, one_b,
                              jnp.where(vmask, h_ref[b, :, nqk + _window(hd)[0]:nqk + _window(hd)[0] + WIN].astype(BF16),
                                        zero_b)) for b, hd in pairs]
            qh = [(h_ref[b, :, hd * ML_DQK_PAD:(hd + 1) * ML_DQK_PAD] * (ML_DQK ** -0.5)).astype(BF16)
                  for b, hd in pairs]
            kh = stack(lambda b, hd: h_ref[b, :, (ML_HEADS + hd) * ML_DQK_PAD:(ML_HEADS + hd + 1) * ML_DQK_PAD])
            bcol = stack(lambda b, hd: bc[b][:, ML_HEADS + hd:ML_HEADS + hd + 1])
            brow = stack(lambda b, hd: bc_t[b][ML_HEADS + hd:ML_HEADS + hd + 1, :])
            icol = stack(lambda b, hd: gates[b][:, hd:hd + 1])
            irow = stack(lambda b, hd: gates_t[b][hd:hd + 1, :])
            m_st = jnp.stack([m_ref[i, 0:1, 0:1] for i in idx])
            c_st = [c_ref[i] for i in idx]

            log_d = jnp.where(causal, bcol + (irow - brow), -jnp.inf)
            inter = bcol + m_st
            m_row = jnp.maximum(inter, jnp.max(log_d, -1, keepdims=True))
            dmat = jnp.exp(log_d - m_row)
            isc = jnp.exp(inter - m_row)
            s = (jnp.stack([_bdot_t1(qh[i], kh[i]) for i in range(n)]) * dmat).astype(BF16)
            num = (jnp.stack([jnp.dot(s[i], vext[i], preferred_element_type=F32) for i in range(n)])
                   + isc * jnp.stack([_bdot(qh[i], c_st[i]) for i in range(n)]))
            den = num[:, :, nlane:nlane + 1]
            hh = num * (1.0 / jnp.maximum(jnp.abs(den), jnp.exp(-m_row)))
            hv = jnp.where(lane == nlane, 0.0, hh)
            ms = jnp.sum(hv * hv, -1, keepdims=True) * (1.0 / ML_DV)
            hnorm = hv * lax.rsqrt(ms + EPS)

            blast = bcol[:, q - 1:q, :]
            log_w = blast - bcol + icol
            m_new = jnp.maximum(blast + m_st, jnp.max(log_w, 1, keepdims=True))
            kw = (kh * jnp.exp(log_w - m_new)).astype(BF16)
            decay = jnp.exp(blast + m_st - m_new)
            for i in range(n):
                c_ref[idx[i]] = decay[i] * c_st[i] + _bdot_t0(kw[i], vext[i])
                m_ref[idx[i]] = jnp.broadcast_to(m_new[i], m_ref.shape[1:])
                hn[pairs[i]] = hnorm[i]

    for b in range(bsz):
        joined = _join_windows([hn[(b, hd)] for hd in range(ML_HEADS)])
        o_sig = h_ref[b, :, nqk + wv:nqk + 2 * wv]
        z_act = h_ref[b, :, nqk + 2 * wv:nqk + 3 * wv]
        o_ref[b] = o_sig * (joined * ng_ref[...]) * z_act


def _mlstm_mixer(h3, gb, ng):
    bsz, seq, n = h3.shape
    wv = ML_HEADS * ML_DV
    blk = lambda c: (0, c, 0)
    return pl.pallas_call(
        functools.partial(_mlstm_kernel, group=ML_GROUP),
        grid=(seq // ML_CHUNK,),
        in_specs=[pl.BlockSpec((bsz, ML_CHUNK, n), blk), _resident((1, LANES)), _resident((1, wv))],
        out_specs=pl.BlockSpec((bsz, ML_CHUNK, wv), blk),
        out_shape=jax.ShapeDtypeStruct((bsz, seq, wv), F32),
        scratch_shapes=[pltpu.VMEM((bsz * ML_HEADS, ML_DQK_PAD, WIN), F32),
                        pltpu.VMEM((bsz * ML_HEADS, TAIL, LANES), F32)],
        compiler_params=pltpu.CompilerParams(dimension_semantics=("arbitrary",),
                                             vmem_limit_bytes=VMEM_LIMIT),
        name="mlstm",
    )(h3, gb, ng)


def _ssd_kernel(h_ref, dtb_ref, alog_ref, dsk_ref, ng_ref, o_ref, s_ref):
    q = SSD_CHUNK
    wx = SSD_HEADS * SSD_HEADDIM
    wbc = SSD_GROUPS * SSD_STATE
    nconv = wx + 2 * wbc
    bsz = h_ref.shape[0]

    @pl.when(pl.program_id(0) == 0)
    def _():
        s_ref[...] = jnp.zeros(s_ref.shape, F32)

    a = -jnp.exp(alog_ref[...])
    row = lax.broadcasted_iota(jnp.int32, (q, q), 0)
    col = lax.broadcasted_iota(jnp.int32, (q, q), 1)
    causal = col <= row
    tril = causal.astype(F32)
    lane = lax.broadcasted_iota(jnp.int32, (1, WIN), 1)
    lo_half = lane[:, :LANES] < SSD_HEADDIM
    zero_b = jnp.zeros((), BF16)

    def per_channel(hcols):
        return jnp.concatenate([jnp.where(lo_half, hcols[..., 2 * p:2 * p + 1], hcols[..., 2 * p + 1:2 * p + 2])
                                for p in range(SSD_HEADS // 2)], axis=-1)

    x = h_ref[:, :, 0:wx]
    dt = _softplus(h_ref[:, :, nconv + wx:nconv + wx + LANES] + dtb_ref[...])
    dta = dt * a
    acum = jnp.stack([jnp.dot(tril, dta[b], precision=HI, preferred_element_type=F32)
                      for b in range(bsz)])
    acum_t = [acum[b].T for b in range(bsz)]
    alast = acum[:, q - 1:q, :]
    eac_c = per_channel(jnp.exp(acum))
    dtx = x * per_channel(dt)
    dtx_dec = dtx * per_channel(jnp.exp(alast - acum))

    ywins = [[] for _ in range(bsz)]
    for g in range(SSD_GROUPS):
        w0, first_valid = _window(g)
        win = slice(w0, w0 + WIN)
        gmask = (lane >= first_valid) if first_valid else (lane < WIN_VALID)
        bm = [h_ref[b, :, wx + g * SSD_STATE:wx + (g + 1) * SSD_STATE].astype(BF16) for b in range(bsz)]
        cm = [h_ref[b, :, wx + wbc + g * SSD_STATE:wx + wbc + (g + 1) * SSD_STATE].astype(BF16) for b in range(bsz)]
        cbm = jnp.stack([_bdot_t1(cm[b], bm[b]) for b in range(bsz)])
        s_prev = [s_ref[b * SSD_GROUPS + g] for b in range(bsz)]
        xw = dtx[:, :, win].astype(BF16)
        xdec = jnp.where(gmask, dtx_dec[:, :, win].astype(BF16), zero_b)
        heads = [g * SSD_RATIO + k for k in range(SSD_RATIO)]
        acol = jnp.concatenate([acum[:, :, hd:hd + 1] for hd in heads], axis=0)
        arow = jnp.stack([acum_t[b][hd:hd + 1, :] for hd in heads for b in range(bsz)])
        lmat = jnp.exp(jnp.where(causal, acol - arow, -jnp.inf))
        mmat = (jnp.concatenate([cbm] * SSD_RATIO, axis=0) * lmat).astype(BF16)
        yoff = jnp.stack([_bdot(cm[b], s_prev[b]) for b in range(bsz)]) * eac_c[:, :, win]
        for b in range(bsz):
            y = yoff[b]
            for k in range(SSD_RATIO):
                lo = first_valid + k * SSD_HEADDIM
                hmask = (lane >= lo) & (lane < lo + SSD_HEADDIM)
                y = y + jnp.dot(mmat[k * bsz + b], jnp.where(hmask, xw[b], zero_b), preferred_element_type=F32)
            ywins[b].append(y)
            s_ref[b * SSD_GROUPS + g] = s_prev[b] * eac_c[b, q - 1:q, win] + _bdot_t0(bm[b], xdec[b])

    for b in range(bsz):
        yz = (_join_windows(ywins[b]) + dsk_ref[...] * x[b]) * h_ref[b, :, nconv:nconv + wx]
        ms = jnp.mean(yz * yz, -1, keepdims=True)
        o_ref[b] = yz * lax.rsqrt(ms + EPS) * ng_ref[...]


def _ssd_mixer(h3, dtb, alog, dsk, ng):
    bsz, seq, n = h3.shape
    wx = SSD_HEADS * SSD_HEADDIM
    blk = lambda c: (0, c, 0)
    return pl.pallas_call(
        _ssd_kernel,
        grid=(seq // SSD_CHUNK,),
        in_specs=[pl.BlockSpec((bsz, SSD_CHUNK, n), blk),
                  _resident((1, LANES)), _resident((1, LANES)),
                  _resident((1, wx)), _resident((1, wx))],
        out_specs=pl.BlockSpec((bsz, SSD_CHUNK, wx), blk),
        out_shape=jax.ShapeDtypeStruct((bsz, seq, wx), F32),
        scratch_shapes=[pltpu.VMEM((bsz * SSD_GROUPS, SSD_STATE, WIN), F32)],
        compiler_params=pltpu.CompilerParams(dimension_semantics=("arbitrary",),
                                             vmem_limit_bytes=VMEM_LIMIT),
        name="ssd",
    )(h3, dtb, alog, dsk, ng)


def _pad_last(a, n):
    return jnp.pad(a, [(0, 0)] * (a.ndim - 1) + [(0, n - a.shape[-1])])


def _pad_heads(a, heads, width, padded):
    s = a.shape[:-1]
    a = a.reshape(s + (heads, width))
    return _pad_last(a, padded).reshape(s + (heads * padded,))


def kernel(x, w_in, w_out, ln_g, ln_b, s5_lambda_re, s5_lambda_im, s5_log_step, s5_b_re, s5_b_im, s5_c_re, s5_c_im, s5_d, s5_w_glu, s5_b_glu, ml_conv_w, ml_conv_b, ml_i_bias, ml_f_bias, ml_norm_g, ssd_conv_w, ssd_conv_b, ssd_dt_bias, ssd_a_log, ssd_d, ssd_norm_g):
    bsz, seq, dm = x.shape
    depth = w_in.shape[0]
    t = bsz * seq
    s5w = s5_d.shape[-1]
    mlw = ML_HEADS * ML_DV
    mlqk = ML_HEADS * ML_DQK
    ssdw = SSD_HEADS * SSD_HEADDIM
    ssdbc = SSD_GROUPS * SSD_STATE
    alpha = (2.0 * depth) ** 0.25
    tm = min(256, seq)

    sizes = (s5w, s5w, 2 * mlqk, mlw, ML_HEADS, ML_HEADS, mlw, mlw, ssdw + 2 * ssdbc, SSD_HEADS, ssdw)
    offs = [0]
    for sz in sizes:
        offs.append(offs[-1] + sz)
    col = lambda i: w_in[:, :, offs[i]:offs[i + 1]]
    w_s5t = jnp.swapaxes(w_in[:, :, :2 * s5w], 1, 2).astype(BF16)
    qkw = col(2)
    w_ml = jnp.concatenate([
        _pad_heads(qkw[..., :mlqk], ML_HEADS, ML_DQK, ML_DQK_PAD),
        _pad_heads(qkw[..., mlqk:], ML_HEADS, ML_DQK, ML_DQK_PAD),
        col(3), col(6), col(7),
        _pad_last(jnp.concatenate([col(4), col(5)], -1), LANES)], -1).astype(BF16)
    w_ssd = jnp.concatenate([col(8), col(10), _pad_last(col(9), LANES)], -1).astype(BF16)

    ml_cw = jnp.concatenate([_pad_heads(ml_conv_w[..., :mlqk], ML_HEADS, ML_DQK, ML_DQK_PAD),
                             _pad_heads(ml_conv_w[..., mlqk:], ML_HEADS, ML_DQK, ML_DQK_PAD)], -1)
    ml_cb = jnp.concatenate([_pad_heads(ml_conv_b[..., :mlqk], ML_HEADS, ML_DQK, ML_DQK_PAD),
                             _pad_heads(ml_conv_b[..., mlqk:], ML_HEADS, ML_DQK, ML_DQK_PAD)], -1)
    ml_gb = _pad_last(jnp.concatenate([ml_i_bias, ml_f_bias], -1), LANES)
    ssd_dtb = _pad_last(ssd_dt_bias, LANES)
    ssd_al = _pad_last(ssd_a_log, LANES)
    ssd_dsk = jnp.repeat(ssd_d, SSD_HEADDIM, axis=-1)
    w_o = w_out.astype(BF16)
    s5_wgt = jnp.swapaxes(s5_w_glu, 1, 2).astype(BF16)
    s5_tabs = _s5_tables(s5_lambda_re, s5_lambda_im, s5_log_step, s5_b_re, s5_b_im, s5_c_re, s5_c_im)

    x2 = x.reshape(t, dm)
    for l in range(depth):
        ht_s5, h_ml, h_ssd = _inproj(x2, w_s5t[l], w_ml[l], w_ssd[l], ml_cw[l], ml_cb[l][None],
                                     ssd_conv_w[l], ssd_conv_b[l][None], tm, seq)
        yt_s5 = _s5_scan(ht_s5, tuple(tb[l] for tb in s5_tabs), bsz)
        y_ml = _mlstm_mixer(h_ml.reshape(bsz, seq, -1), ml_gb[l][None], ml_norm_g[l][None])
        y_ssd = _ssd_mixer(h_ssd.reshape(bsz, seq, -1), ssd_dtb[l][None], ssd_al[l][None],
                           ssd_dsk[l][None], ssd_norm_g[l][None])
        x2 = _outproj(x2, yt_s5, ht_s5, y_ml.reshape(t, mlw), y_ssd.reshape(t, ssdw),
                      s5_d[l][:, None], s5_wgt[l], s5_b_glu[l][:, None],
                      w_o[l, :s5w], w_o[l, s5w:s5w + mlw], w_o[l, s5w + mlw:],
                      ln_g[l][None], ln_b[l][None], min(OUT_TILE, seq), alpha)
    return x2.reshape(bsz, seq, dm)
```

```python
import functools
import math

import jax
import jax.numpy as jnp
from jax import lax
from jax.experimental import pallas as pl
from jax.experimental.pallas import tpu as pltpu

F32 = jnp.float32
BF16 = jnp.bfloat16
HI = lax.Precision.HIGHEST

LANES = 128
TAIL = 8
VMEM_LIMIT = 56 * 1024 * 1024

CONV_K = 4
OUT_TILE = 512
OUT_SUB = 256
EPS = 1e-5

S5_GROUP = 16
S5_STATE = 64
S5_CHUNK = LANES

ML_HEADS = 4
ML_DV = 192
ML_DQK = 96
ML_DQK_PAD = 128
ML_CHUNK = 128
ML_GROUP = 8

SSD_HEADDIM = 64
SSD_HEADS = 12
SSD_GROUPS = 4
SSD_RATIO = SSD_HEADS // SSD_GROUPS
SSD_STATE = 128
SSD_CHUNK = 128

WIN = 256
WIN_VALID = 192
assert ML_DV == WIN_VALID and SSD_RATIO * SSD_HEADDIM == WIN_VALID


def _silu(x):
    return x * jax.nn.sigmoid(x)


def _softplus(x):
    return jnp.maximum(x, 0.0) + jnp.log1p(jnp.exp(-jnp.abs(x)))


def _bdot(a, b):
    return jnp.dot(a.astype(BF16), b.astype(BF16), preferred_element_type=F32)


def _bdot_t0(a, b):
    return lax.dot_general(a.astype(BF16), b.astype(BF16), (((0,), (0,)), ((), ())),
                           preferred_element_type=F32)


def _bdot_t1(a, b):
    return lax.dot_general(a.astype(BF16), b.astype(BF16), (((1,), (1,)), ((), ())),
                           preferred_element_type=F32)


def _resident(shape):
    zeros = (0,) * len(shape)
    return pl.BlockSpec(shape, lambda *_: zeros, pipeline_mode=pl.Buffered(1))


def _resident_layer(shape, layer):
    idx = (layer,) + (0,) * (len(shape) - 1)
    return pl.BlockSpec((None,) + tuple(shape[1:]), lambda *_: idx, pipeline_mode=pl.Buffered(1))


def _window(idx):
    start = (idx // 2) * 2 * WIN_VALID + (LANES if idx % 2 else 0)
    return start, (WIN - WIN_VALID if idx % 2 else 0)


def _join_windows(wins):
    tiles = []
    for pr in range(len(wins) // 2):
        ev, od = wins[2 * pr], wins[2 * pr + 1]
        tiles += [ev[:, :LANES], ev[:, LANES:] + od[:, :LANES], od[:, LANES:]]
    return jnp.concatenate(tiles, axis=1)


def _inproj_kernel(x_ref, ws5_ref, wml_ref, wssd_ref, mcw_ref, mcb_ref, scw_ref, scb_ref,
                   os5_ref, oml_ref, ossd_ref, mtail_ref, stail_ref, mbuf_ref, sbuf_ref,
                   *, tn, tiles_per_seq, ml_segs, ssd_segs):
    tm = x_ref.shape[0]

    @pl.when(pl.program_id(0) % tiles_per_seq == 0)
    def _():
        mtail_ref[...] = jnp.zeros(mtail_ref.shape, F32)
        stail_ref[...] = jnp.zeros(stail_ref.shape, F32)

    xb = x_ref[...].astype(BF16)
    hs5 = lax.dot_general(ws5_ref[...], xb, (((1,), (1,)), ((), ())), preferred_element_type=F32)
    half = os5_ref.shape[0] // 2
    os5_ref[0:half, :] = hs5[0:half, :]
    os5_ref[half:, :] = _silu(hs5[half:, :])

    def conv_silu(w_ref, o_ref, cw_ref, cb_ref, tail_ref, buf_ref, width):
        for j in range(0, width, tn):
            e = min(j + tn, width)
            r = jnp.dot(xb, w_ref[:, j:e], preferred_element_type=F32)
            buf_ref[0:TAIL, j:e] = tail_ref[:, j:e]
            buf_ref[TAIL:TAIL + tm, j:e] = r
            acc = cb_ref[:, j:e] + cw_ref[CONV_K - 1:CONV_K, j:e] * r
            for k in range(CONV_K - 1):
                off = TAIL - (CONV_K - 1) + k
                acc = acc + cw_ref[k:k + 1, j:e] * buf_ref[off:off + tm, j:e]
            tail_ref[:, j:e] = r[tm - TAIL:tm, :]
            o_ref[:, j:e] = _silu(acc)

    def plain(w_ref, o_ref, lo, hi, fn):
        for j in range(lo, hi, tn):
            e = min(j + tn, hi)
            r = jnp.dot(xb, w_ref[:, j:e], preferred_element_type=F32)
            o_ref[:, j:e] = r if fn is None else fn(r)

    conv_silu(wml_ref, oml_ref, mcw_ref, mcb_ref, mtail_ref, mbuf_ref, ml_segs[0][0])
    for lo, hi, fn in ml_segs[1:]:
        plain(wml_ref, oml_ref, lo, hi, fn)
    conv_silu(wssd_ref, ossd_ref, scw_ref, scb_ref, stail_ref, sbuf_ref, ssd_segs[0][0])
    for lo, hi, fn in ssd_segs[1:]:
        plain(wssd_ref, ossd_ref, lo, hi, fn)


def _inproj(x2, ws5t, wml, wssd, layer, mcw, mcb, scw, scb, tm, seq):
    t, k = x2.shape
    ns5, nml, nssd = ws5t.shape[1], wml.shape[2], wssd.shape[2]
    nqk, nconv = mcw.shape[1], scw.shape[1]
    wv = ML_HEADS * ML_DV
    wx = SSD_HEADS * SSD_HEADDIM
    ml_segs = ((nqk,), (nqk, nqk + wv, None), (nqk + wv, nqk + 2 * wv, jax.nn.sigmoid),
               (nqk + 2 * wv, nqk + 3 * wv, _silu), (nqk + 3 * wv, nml, None))
    ssd_segs = ((nconv,), (nconv, nconv + wx, _silu), (nconv + wx, nssd, None))
    return pl.pallas_call(
        functools.partial(_inproj_kernel, tn=512, tiles_per_seq=seq // tm, ml_segs=ml_segs, ssd_segs=ssd_segs),
        grid=(t // tm,),
        in_specs=[pl.BlockSpec((tm, k), lambda i: (i, 0)),
                  _resident_layer(ws5t.shape, layer), _resident_layer(wml.shape, layer),
                  _resident_layer(wssd.shape, layer),
                  _resident(mcw.shape), _resident(mcb.shape), _resident(scw.shape), _resident(scb.shape)],
        out_specs=[pl.BlockSpec((ns5, tm), lambda i: (0, i)),
                   pl.BlockSpec((tm, nml), lambda i: (i, 0)),
                   pl.BlockSpec((tm, nssd), lambda i: (i, 0))],
        out_shape=[jax.ShapeDtypeStruct((ns5, t), F32),
                   jax.ShapeDtypeStruct((t, nml), F32),
                   jax.ShapeDtypeStruct((t, nssd), F32)],
        scratch_shapes=[pltpu.VMEM((TAIL, nqk), F32), pltpu.VMEM((TAIL, nconv), F32),
                        pltpu.VMEM((TAIL + tm, nqk), F32), pltpu.VMEM((TAIL + tm, nconv), F32)],
        compiler_params=pltpu.CompilerParams(dimension_semantics=("arbitrary",),
                                             vmem_limit_bytes=VMEM_LIMIT),
        name="inproj",
    )(x2, ws5t, wml, wssd, mcw, mcb, scw, scb)


def _outproj_kernel(x_ref, y_ref, u_ref, z_ref, b_ref, c_ref, d_ref, wg_ref, bg_ref,
                    wo_ref, g_ref, bb_ref, o_ref, *, alpha, sub):
    c0 = math.sqrt(2.0 / math.pi)
    na, nb = y_ref.shape[0], b_ref.shape[1]
    for r0 in range(0, x_ref.shape[0], sub):
        rows = slice(r0, r0 + sub)
        ys = y_ref[:, rows] + d_ref[...] * u_ref[:, rows]
        gl = 0.5 * ys * (1.0 + jnp.tanh(c0 * (ys + 0.044715 * (ys * ys * ys))))
        a = gl * jax.nn.sigmoid(_bdot(wg_ref[...], gl) + bg_ref[...]) * z_ref[:, rows]
        out = (_bdot_t0(a, wo_ref[0:na, :]) + _bdot(b_ref[rows, :], wo_ref[na:na + nb, :])
               + _bdot(c_ref[rows, :], wo_ref[na + nb:, :]))
        y = alpha * x_ref[rows, :] + out
        mu = jnp.mean(y, -1, keepdims=True)
        d = y - mu
        var = jnp.mean(d * d, -1, keepdims=True)
        o_ref[rows, :] = d * lax.rsqrt(var + EPS) * g_ref[...] + bb_ref[...]


def _outproj(x2, yt, ht, mb, mc, d5, wgt, bg5, wo, layer, g, b, tm, alpha):
    t, d = x2.shape
    wd = yt.shape[0]
    row = lambda i: (i, 0)
    return pl.pallas_call(
        functools.partial(_outproj_kernel, alpha=alpha, sub=min(OUT_SUB, tm)),
        grid=(t // tm,),
        in_specs=[pl.BlockSpec((tm, d), row),
                  pl.BlockSpec((wd, tm), lambda i: (0, i)),
                  pl.BlockSpec((wd, tm), lambda i: (0, i)),
                  pl.BlockSpec((wd, tm), lambda i: (1, i)),
                  pl.BlockSpec((tm, mb.shape[1]), row),
                  pl.BlockSpec((tm, mc.shape[1]), row),
                  _resident((wd, 1)), _resident((wd, wd)), _resident((wd, 1)),
                  _resident_layer(wo.shape, layer),
                  _resident((1, d)), _resident((1, d))],
        out_specs=pl.BlockSpec((tm, d), row),
        out_shape=jax.ShapeDtypeStruct((t, d), F32),
        compiler_params=pltpu.CompilerParams(dimension_semantics=("arbitrary",),
                                             vmem_limit_bytes=VMEM_LIMIT),
        name="outproj_ln",
    )(x2, yt, ht, ht, mb, mc, d5, wgt, bg5, wo, g, b)


def _s5_scan_kernel(u_ref, ptab_ref, stab_ref, ctab_ref, bbrow_ref, bbt_ref, rot_ref, y_ref,
                    tta_ref, ttb_ref, kk_ref, up_ref, acc_ref, sloc_ref, sin_ref, *, bsz, nchunk):
    q = S5_CHUNK
    p2 = 2 * S5_STATE
    ng = S5_GROUP
    srow = lax.broadcasted_iota(jnp.int32, (q, q), 0)
    tcol = lax.broadcasted_iota(jnp.int32, (q, q), 1)
    upper = tcol >= srow

    ctab = ctab_ref[0]

    def c_lam(co, a, b):
        return ctab[:, co:co + 1] * a + ctab[:, ng + co:ng + co + 1] * b

    cp0 = jnp.concatenate([c_lam(co, ptab_ref[0, 0], ptab_ref[0, 1]) for co in range(ng)], axis=1)
    vt = jnp.concatenate([c_lam(co, ptab_ref[0, 2], ptab_ref[0, 3]) for co in range(ng)], axis=1).astype(BF16)
    kk = jnp.dot(bbt_ref[0], cp0, precision=HI, preferred_element_type=F32)
    for co in range(ng):
        kk_ref[co] = kk[:, co * q:(co + 1) * q]

    u3 = u_ref[...].reshape(ng, bsz * nchunk, q)
    for pr in range(ng // 2):
        up_ref[pr] = jnp.concatenate([u3[2 * pr], u3[2 * pr + 1]], axis=1).astype(BF16)
    rs, ims = stab_ref[0, 0], stab_ref[0, 1]
    sloc = None
    for pr in range(ng // 2):
        wt = jnp.concatenate(
            [(rs * bbrow_ref[0, ci:ci + 1, :] + ims * bbrow_ref[0, ng + ci:ng + ci + 1, :]).astype(BF16)
             for ci in (2 * pr, 2 * pr + 1)], axis=0)
        part = jnp.dot(up_ref[pr], wt, preferred_element_type=F32)
        sloc = part if sloc is None else sloc + part
    sloc_ref[...] = sloc
    a1 = rot_ref[0, 0:1, :]
    a2 = rot_ref[0, 1:2, :]

    def body(c, states):
        nxt = []
        for b in range(bsz):
            r = pl.ds(b * nchunk + c, 1)
            s = states[b]
            sin_ref[r, :] = s
            swapped = jnp.concatenate([s[:, p2:], s[:, :p2]], axis=1)
            nxt.append(a1 * s + a2 * swapped + sloc_ref[r, :])
        return tuple(nxt)

    lax.fori_loop(0, nchunk, body, tuple(jnp.zeros((1, 2 * p2), F32) for _ in range(bsz)))
    acc_ref[...] = jnp.dot(sin_ref[:, 0:p2].astype(BF16), vt, preferred_element_type=F32)

    def build_pair(pr, t_ref):
        for k in range(2):
            for co in range(ng):
                lag = kk_ref[co, pl.ds(2 * pr + k, 1), :]
                blk = pltpu.roll(jnp.broadcast_to(lag, (q, q)), 0, 1, stride=1, stride_axis=0)
                t_ref[k * q:(k + 1) * q, co * q:(co + 1) * q] = jnp.where(upper, blk, 0.0).astype(BF16)

    def dot_pair(pr, t_ref):
        acc_ref[...] += jnp.dot(up_ref[pr], t_ref[...], preferred_element_type=F32)

    npair = ng // 2
    build_pair(0, tta_ref)

    def step(j, carry):
        build_pair(2 * j + 1, ttb_ref)
        dot_pair(2 * j, tta_ref)
        build_pair(2 * j + 2, tta_ref)
        dot_pair(2 * j + 1, ttb_ref)
        return carry

    lax.fori_loop(0, npair // 2 - 1, step, 0)
    build_pair(npair - 1, ttb_ref)
    dot_pair(npair - 2, tta_ref)
    dot_pair(npair - 1, ttb_ref)
    y = acc_ref[...]
    y_ref[...] = jnp.stack([y[:, co * q:(co + 1) * q] for co in range(ng)]).reshape(ng, bsz * nchunk * q)


def _s5_scan(u3, tabs, layer, bsz):
    ptab, stab, ctab, bbrow, bbt, rot = tabs
    q = S5_CHUNK
    w = u3.shape[0] // 2
    n = u3.shape[1] // q
    g = w // S5_GROUP
    qc = S5_GROUP * q
    p2 = 2 * S5_STATE
    def tab(a):
        zeros = (0,) * (a.ndim - 2)
        return pl.BlockSpec((None, 1) + tuple(a.shape[2:]), lambda i: (layer, i) + zeros)

    return pl.pallas_call(
        functools.partial(_s5_scan_kernel, bsz=bsz, nchunk=n // bsz),
        grid=(g,),
        in_specs=[pl.BlockSpec((S5_GROUP, n * q), lambda i: (i, 0)),
                  tab(ptab), tab(stab), tab(ctab), tab(bbrow), tab(bbt), tab(rot)],
        out_specs=pl.BlockSpec((S5_GROUP, n * q), lambda i: (i, 0)),
        out_shape=jax.ShapeDtypeStruct((w, n * q), F32),
        scratch_shapes=[pltpu.VMEM((2 * q, qc), BF16), pltpu.VMEM((2 * q, qc), BF16),
                        pltpu.VMEM((S5_GROUP, S5_GROUP, q), F32),
                        pltpu.VMEM((S5_GROUP // 2, n, 2 * q), BF16), pltpu.VMEM((n, qc), F32),
                        pltpu.VMEM((n, 2 * p2), F32), pltpu.VMEM((n, 2 * p2), F32)],
        compiler_params=pltpu.CompilerParams(dimension_semantics=("arbitrary",),
                                             vmem_limit_bytes=VMEM_LIMIT),
        name="s5_scan",
    )(u3, ptab, stab, ctab, bbrow, bbt, rot)


def _s5_tables(lam_re, lam_im, log_step, b_re, b_im, c_re, c_im):
    q = S5_CHUNK
    step = jnp.exp(log_step)[..., None]
    zr, zi = lam_re * step, lam_im * step
    er = jnp.exp(zr)
    nr, ni = er * jnp.cos(zi) - 1.0, er * jnp.sin(zi)
    den = lam_re * lam_re + lam_im * lam_im
    fr = (nr * lam_re + ni * lam_im) / den
    fi = (ni * lam_re - nr * lam_im) / den
    bbr = fr[..., None] * b_re - fi[..., None] * b_im
    bbi = fr[..., None] * b_im + fi[..., None] * b_re

    def powers(e, zre, zim):
        m = jnp.exp(e * zre)
        return m * jnp.cos(e * zim), m * jnp.sin(e * zim)

    lag = jnp.arange(q, dtype=F32)
    pr0, pi0 = powers(lag, zr[..., None], zi[..., None])
    pr1, pi1 = powers(lag + 1.0, zr[..., None], zi[..., None])
    cat = lambda a, b: jnp.concatenate([a, b], axis=2)
    ptab = jnp.stack([cat(pr0, -pi0), cat(-pi0, -pr0), cat(pr1, -pi1), cat(-pi1, -pr1)], axis=2)
    rs, ims = powers((q - 1.0 - lag)[:, None], zr[:, :, None, :], zi[:, :, None, :])
    stab = jnp.stack([jnp.concatenate([rs] * 4, -1), jnp.concatenate([-ims, ims, ims, -ims], -1)], axis=2)
    tr = lambda a: jnp.swapaxes(a, 2, 3)
    ctab = jnp.concatenate([tr(jnp.concatenate([c_re, c_re], -1)), tr(jnp.concatenate([c_im, c_im], -1))], -1)
    bbrow = jnp.concatenate([tr(jnp.concatenate([bbr, bbi, bbi, bbr], 2)),
                             tr(jnp.concatenate([bbi, bbr, bbr, bbi], 2))], 2)
    bbt = tr(jnp.concatenate([bbr, bbi], 2))
    prq, piq = powers(float(q), zr, zi)
    rot = jnp.stack([jnp.concatenate([prq] * 4, -1),
                     jnp.concatenate([-piq, piq, piq, -piq], -1)], axis=2)
    return ptab, stab, ctab, bbrow, bbt, rot


def _mlstm_kernel(h_ref, gb_ref, ng_ref, o_ref, c_ref, m_ref, *, group):
    q = ML_CHUNK
    nqk = 2 * ML_HEADS * ML_DQK_PAD
    wv = ML_HEADS * ML_DV
    bsz = h_ref.shape[0]

    @pl.when(pl.program_id(0) == 0)
    def _():
        c_ref[...] = jnp.zeros(c_ref.shape, F32)
        m_ref[...] = jnp.zeros(m_ref.shape, F32)

    row = lax.broadcasted_iota(jnp.int32, (q, q), 0)
    col = lax.broadcasted_iota(jnp.int32, (q, q), 1)
    causal = col <= row
    tril = causal.astype(F32)
    lane = lax.broadcasted_iota(jnp.int32, (1, WIN), 1)
    one_b, zero_b = jnp.ones((), BF16), jnp.zeros((), BF16)

    gates, bc, gates_t, bc_t = [], [], [], []
    for b in range(bsz):
        g = h_ref[b, :, nqk + 3 * wv:nqk + 3 * wv + LANES] + gb_ref[...]
        c = jnp.dot(tril, -_softplus(-g), precision=HI, preferred_element_type=F32)
        gates.append(g)
        bc.append(c)
        gates_t.append(g.T)
        bc_t.append(c.T)

    hn = {}
    for par in (0, 1):
        allp = [(b, hd) for hd in range(par, ML_HEADS, 2) for b in range(bsz)]
        first_valid = _window(par)[1]
        vmask = (lane >= first_valid) if first_valid else (lane < WIN_VALID)
        nlane = 0 if first_valid else WIN_VALID
        for g0 in range(0, len(allp), group):
            pairs = allp[g0:g0 + group]
            idx = [b * ML_HEADS + hd for b, hd in pairs]
            n = len(pairs)

            def stack(fn):
                return jnp.stack([fn(b, hd) for b, hd in pairs])

            vext = [jnp.where(lane == nlane, one_b,
                              jnp.where(vmask, h_ref[b, :, nqk + _window(hd)[0]:nqk + _window(hd)[0] + WIN].astype(BF16),
                                        zero_b)) for b, hd in pairs]
            qh = [(h_ref[b, :, hd * ML_DQK_PAD:(hd + 1) * ML_DQK_PAD] * (ML_DQK ** -0.5)).astype(BF16)
                  for b, hd in pairs]
            kh = stack(lambda b, hd: h_ref[b, :, (ML_HEADS + hd) * ML_DQK_PAD:(ML_HEADS + hd + 1) * ML_DQK_PAD])
            bcol = stack(lambda b, hd: bc[b][:, ML_HEADS + hd:ML_HEADS + hd + 1])
            brow = stack(lambda b, hd: bc_t[b][ML_HEADS + hd:ML_HEADS + hd + 1, :])
            icol = stack(lambda b, hd: gates[b][:, hd:hd + 1])
            irow = stack(lambda b, hd: gates_t[b][hd:hd + 1, :])
            m_st = jnp.stack([m_ref[i, 0:1, 0:1] for i in idx])
            c_st = [c_ref[i] for i in idx]

            log_d = jnp.where(causal, bcol + (irow - brow), -jnp.inf)
            inter = bcol + m_st
            m_row = jnp.maximum(inter, jnp.max(log_d, -1, keepdims=True))
            dmat = jnp.exp(log_d - m_row)
            isc = jnp.exp(inter - m_row)
            s = (jnp.stack([_bdot_t1(qh[i], kh[i]) for i in range(n)]) * dmat).astype(BF16)
            num = (jnp.stack([jnp.dot(s[i], vext[i], preferred_element_type=F32) for i in range(n)])
                   + isc * jnp.stack([_bdot(qh[i], c_st[i]) for i in range(n)]))
            den = num[:, :, nlane:nlane + 1]
            hh = num * (1.0 / jnp.maximum(jnp.abs(den), jnp.exp(-m_row)))
            hv = jnp.where(lane == nlane, 0.0, hh)
            ms = jnp.sum(hv * hv, -1, keepdims=True) * (1.0 / ML_DV)
            hnorm = hv * lax.rsqrt(ms + EPS)

            blast = bcol[:, q - 1:q, :]
            log_w = blast - bcol + icol
            m_new = jnp.maximum(blast + m_st, jnp.max(log_w, 1, keepdims=True))
            kw = (kh * jnp.exp(log_w - m_new)).astype(BF16)
            decay = jnp.exp(blast + m_st - m_new)
            for i in range(n):
                c_ref[idx[i]] = decay[i] * c_st[i] + _bdot_t0(kw[i], vext[i])
                m_ref[idx[i]] = jnp.broadcast_to(m_new[i], m_ref.shape[1:])
                hn[pairs[i]] = hnorm[i]

    for b in range(bsz):
        joined = _join_windows([hn[(b, hd)] for hd in range(ML_HEADS)])
        o_sig = h_ref[b, :, nqk + wv:nqk + 2 * wv]
        z_act = h_ref[b, :, nqk + 2 * wv:nqk + 3 * wv]
        o_ref[b] = o_sig * (joined * ng_ref[...]) * z_act


def _mlstm_mixer(h3, gb, ng):
    bsz, seq, n = h3.shape
    wv = ML_HEADS * ML_DV
    blk = lambda c: (0, c, 0)
    return pl.pallas_call(
        functools.partial(_mlstm_kernel, group=ML_GROUP),
        grid=(seq // ML_CHUNK,),
        in_specs=[pl.BlockSpec((bsz, ML_CHUNK, n), blk), _resident((1, LANES)), _resident((1, wv))],
        out_specs=pl.BlockSpec((bsz, ML_CHUNK, wv), blk),
        out_shape=jax.ShapeDtypeStruct((bsz, seq, wv), F32),
        scratch_shapes=[pltpu.VMEM((bsz * ML_HEADS, ML_DQK_PAD, WIN), F32),
                        pltpu.VMEM((bsz * ML_HEADS, TAIL, LANES), F32)],
        compiler_params=pltpu.CompilerParams(dimension_semantics=("arbitrary",),
                                             vmem_limit_bytes=VMEM_LIMIT),
        name="mlstm",
    )(h3, gb, ng)


def _ssd_kernel(h_ref, dtb_ref, alog_ref, dsk_ref, ng_ref, o_ref, s_ref):
    q = SSD_CHUNK
    wx = SSD_HEADS * SSD_HEADDIM
    wbc = SSD_GROUPS * SSD_STATE
    nconv = wx + 2 * wbc
    bsz = h_ref.shape[0]

    @pl.when(pl.program_id(0) == 0)
    def _():
        s_ref[...] = jnp.zeros(s_ref.shape, F32)

    a = -jnp.exp(alog_ref[...])
    row = lax.broadcasted_iota(jnp.int32, (q, q), 0)
    col = lax.broadcasted_iota(jnp.int32, (q, q), 1)
    causal = col <= row
    tril = causal.astype(F32)
    lane = lax.broadcasted_iota(jnp.int32, (1, WIN), 1)
    lo_half = lane[:, :LANES] < SSD_HEADDIM
    zero_b = jnp.zeros((), BF16)

    def per_channel(hcols):
        return jnp.concatenate([jnp.where(lo_half, hcols[..., 2 * p:2 * p + 1], hcols[..., 2 * p + 1:2 * p + 2])
                                for p in range(SSD_HEADS // 2)], axis=-1)

    x = h_ref[:, :, 0:wx]
    dt = _softplus(h_ref[:, :, nconv + wx:nconv + wx + LANES] + dtb_ref[...])
    dta = dt * a
    acum = jnp.stack([jnp.dot(tril, dta[b], precision=HI, preferred_element_type=F32)
                      for b in range(bsz)])
    acum_t = [acum[b].T for b in range(bsz)]
    alast = acum[:, q - 1:q, :]
    eac_c = per_channel(jnp.exp(acum))
    dtx = x * per_channel(dt)
    dtx_dec = dtx * per_channel(jnp.exp(alast - acum))

    ywins = [[] for _ in range(bsz)]
    for g in range(SSD_GROUPS):
        w0, first_valid = _window(g)
        win = slice(w0, w0 + WIN)
        gmask = (lane >= first_valid) if first_valid else (lane < WIN_VALID)
        bm = [h_ref[b, :, wx + g * SSD_STATE:wx + (g + 1) * SSD_STATE].astype(BF16) for b in range(bsz)]
        cm = [h_ref[b, :, wx + wbc + g * SSD_STATE:wx + wbc + (g + 1) * SSD_STATE].astype(BF16) for b in range(bsz)]
        cbm = jnp.stack([_bdot_t1(cm[b], bm[b]) for b in range(bsz)])
        s_prev = [s_ref[b * SSD_GROUPS + g] for b in range(bsz)]
        xw = dtx[:, :, win].astype(BF16)
        xdec = jnp.where(gmask, dtx_dec[:, :, win].astype(BF16), zero_b)
        mmat = []
        for k in range(SSD_RATIO):
            hd = g * SSD_RATIO + k
            acol = acum[:, :, hd:hd + 1]
            arow = jnp.stack([acum_t[b][hd:hd + 1, :] for b in range(bsz)])
            mmat.append((cbm * jnp.exp(jnp.where(causal, acol - arow, -jnp.inf))).astype(BF16))
        yoff = jnp.stack([_bdot(cm[b], s_prev[b]) for b in range(bsz)]) * eac_c[:, :, win]
        for b in range(bsz):
            y = yoff[b]
            for k in range(SSD_RATIO):
                lo = first_valid + k * SSD_HEADDIM
                hmask = (lane >= lo) & (lane < lo + SSD_HEADDIM)
                y = y + jnp.dot(mmat[k][b], jnp.where(hmask, xw[b], zero_b), preferred_element_type=F32)
            ywins[b].append(y)
            s_ref[b * SSD_GROUPS + g] = s_prev[b] * eac_c[b, q - 1:q, win] + _bdot_t0(bm[b], xdec[b])

    for b in range(bsz):
        yz = (_join_windows(ywins[b]) + dsk_ref[...] * x[b]) * h_ref[b, :, nconv:nconv + wx]
        ms = jnp.mean(yz * yz, -1, keepdims=True)
        o_ref[b] = yz * lax.rsqrt(ms + EPS) * ng_ref[...]


def _ssd_mixer(h3, dtb, alog, dsk, ng):
    bsz, seq, n = h3.shape
    wx = SSD_HEADS * SSD_HEADDIM
    blk = lambda c: (0, c, 0)
    return pl.pallas_call(
        _ssd_kernel,
        grid=(seq // SSD_CHUNK,),
        in_specs=[pl.BlockSpec((bsz, SSD_CHUNK, n), blk),
                  _resident((1, LANES)), _resident((1, LANES)),
                  _resident((1, wx)), _resident((1, wx))],
        out_specs=pl.BlockSpec((bsz, SSD_CHUNK, wx), blk),
        out_shape=jax.ShapeDtypeStruct((bsz, seq, wx), F32),
        scratch_shapes=[pltpu.VMEM((bsz * SSD_GROUPS, SSD_STATE, WIN), F32)],
        compiler_params=pltpu.CompilerParams(dimension_semantics=("arbitrary",),
                                             vmem_limit_bytes=VMEM_LIMIT),
        name="ssd",
    )(h3, dtb, alog, dsk, ng)


def _pad_last(a, n):
    return jnp.pad(a, [(0, 0)] * (a.ndim - 1) + [(0, n - a.shape[-1])])


def _pad_heads(a, heads, width, padded):
    s = a.shape[:-1]
    a = a.reshape(s + (heads, width))
    return _pad_last(a, padded).reshape(s + (heads * padded,))


def kernel(x, w_in, w_out, ln_g, ln_b, s5_lambda_re, s5_lambda_im, s5_log_step, s5_b_re, s5_b_im, s5_c_re, s5_c_im, s5_d, s5_w_glu, s5_b_glu, ml_conv_w, ml_conv_b, ml_i_bias, ml_f_bias, ml_norm_g, ssd_conv_w, ssd_conv_b, ssd_dt_bias, ssd_a_log, ssd_d, ssd_norm_g):
    bsz, seq, dm = x.shape
    depth = w_in.shape[0]
    t = bsz * seq
    s5w = s5_d.shape[-1]
    mlw = ML_HEADS * ML_DV
    mlqk = ML_HEADS * ML_DQK
    ssdw = SSD_HEADS * SSD_HEADDIM
    ssdbc = SSD_GROUPS * SSD_STATE
    alpha = (2.0 * depth) ** 0.25
    tm = min(256, seq)

    sizes = (s5w, s5w, 2 * mlqk, mlw, ML_HEADS, ML_HEADS, mlw, mlw, ssdw + 2 * ssdbc, SSD_HEADS, ssdw)
    offs = [0]
    for sz in sizes:
        offs.append(offs[-1] + sz)
    col = lambda i: w_in[:, :, offs[i]:offs[i + 1]]
    w_s5t = jnp.swapaxes(w_in[:, :, :2 * s5w], 1, 2).astype(BF16)
    qkw = col(2)
    w_ml = jnp.concatenate([
        _pad_heads(qkw[..., :mlqk], ML_HEADS, ML_DQK, ML_DQK_PAD),
        _pad_heads(qkw[..., mlqk:], ML_HEADS, ML_DQK, ML_DQK_PAD),
        col(3), col(6), col(7),
        _pad_last(jnp.concatenate([col(4), col(5)], -1), LANES)], -1).astype(BF16)
    w_ssd = jnp.concatenate([col(8), col(10), _pad_last(col(9), LANES)], -1).astype(BF16)

    ml_cw = jnp.concatenate([_pad_heads(ml_conv_w[..., :mlqk], ML_HEADS, ML_DQK, ML_DQK_PAD),
                             _pad_heads(ml_conv_w[..., mlqk:], ML_HEADS, ML_DQK, ML_DQK_PAD)], -1)
    ml_cb = jnp.concatenate([_pad_heads(ml_conv_b[..., :mlqk], ML_HEADS, ML_DQK, ML_DQK_PAD),
                             _pad_heads(ml_conv_b[..., mlqk:], ML_HEADS, ML_DQK, ML_DQK_PAD)], -1)
    ml_gb = _pad_last(jnp.concatenate([ml_i_bias, ml_f_bias], -1), LANES)
    ssd_dtb = _pad_last(ssd_dt_bias, LANES)
    ssd_al = _pad_last(ssd_a_log, LANES)
    ssd_dsk = jnp.repeat(ssd_d, SSD_HEADDIM, axis=-1)
    w_o = w_out.astype(BF16)
    s5_wgt = jnp.swapaxes(s5_w_glu, 1, 2).astype(BF16)
    s5_tabs = _s5_tables(s5_lambda_re, s5_lambda_im, s5_log_step, s5_b_re, s5_b_im, s5_c_re, s5_c_im)

    x2 = x.reshape(t, dm)
    for l in range(depth):
        ht_s5, h_ml, h_ssd = _inproj(x2, w_s5t, w_ml, w_ssd, l, ml_cw[l], ml_cb[l][None],
                                     ssd_conv_w[l], ssd_conv_b[l][None], tm, seq)
        yt_s5 = _s5_scan(ht_s5, s5_tabs, l, bsz)
        y_ml = _mlstm_mixer(h_ml.reshape(bsz, seq, -1), ml_gb[l][None], ml_norm_g[l][None])
        y_ssd = _ssd_mixer(h_ssd.reshape(bsz, seq, -1), ssd_dtb[l][None], ssd_al[l][None],
                           ssd_dsk[l][None], ssd_norm_g[l][None])
        x2 = _outproj(x2, yt_s5, ht_s5, y_ml.reshape(t, mlw), y_ssd.reshape(t, ssdw),
                      s5_d[l][:, None], s5_wgt[l], s5_b_glu[l][:, None],
                      w_o, l,
                      ln_g[l][None], ln_b[l][None], min(OUT_TILE, seq), alpha)
    return x2.reshape(bsz, seq, dm)
```

```python
import functools
import math

import jax
import jax.numpy as jnp
from jax import lax
from jax.experimental import pallas as pl
from jax.experimental.pallas import tpu as pltpu

F32 = jnp.float32
BF16 = jnp.bfloat16
HI = lax.Precision.HIGHEST

LANES = 128
TAIL = 8
VMEM_LIMIT = 56 * 1024 * 1024

CONV_K = 4
OUT_TILE = 512
OUT_SUB = 256
EPS = 1e-5

S5_GROUP = 16
S5_STATE = 64
S5_CHUNK = LANES

ML_HEADS = 4
ML_DV = 192
ML_DQK = 96
ML_DQK_PAD = 128
ML_CHUNK = 128
ML_GROUP = 8

SSD_HEADDIM = 64
SSD_HEADS = 12
SSD_GROUPS = 4
SSD_RATIO = SSD_HEADS // SSD_GROUPS
SSD_STATE = 128
SSD_CHUNK = 128

ML_GATE_I = SSD_HEADS
ML_GATE_F = SSD_HEADS + ML_HEADS
assert ML_CHUNK == SSD_CHUNK and ML_GATE_F + ML_HEADS <= LANES

WIN = 256
WIN_VALID = 192
assert ML_DV == WIN_VALID and SSD_RATIO * SSD_HEADDIM == WIN_VALID


def _silu(x):
    return x * jax.nn.sigmoid(x)


def _softplus(x):
    return jnp.maximum(x, 0.0) + jnp.log1p(jnp.exp(-jnp.abs(x)))


def _bdot(a, b):
    return jnp.dot(a.astype(BF16), b.astype(BF16), preferred_element_type=F32)


def _bdot_t0(a, b):
    return lax.dot_general(a.astype(BF16), b.astype(BF16), (((0,), (0,)), ((), ())),
                           preferred_element_type=F32)


def _bdot_t1(a, b):
    return lax.dot_general(a.astype(BF16), b.astype(BF16), (((1,), (1,)), ((), ())),
                           preferred_element_type=F32)


def _resident(shape):
    zeros = (0,) * len(shape)
    return pl.BlockSpec(shape, lambda *_: zeros, pipeline_mode=pl.Buffered(1))


def _resident_layer(shape, layer):
    idx = (layer,) + (0,) * (len(shape) - 1)
    return pl.BlockSpec((None,) + tuple(shape[1:]), lambda *_: idx, pipeline_mode=pl.Buffered(1))


def _window(idx):
    start = (idx // 2) * 2 * WIN_VALID + (LANES if idx % 2 else 0)
    return start, (WIN - WIN_VALID if idx % 2 else 0)


def _join_windows(wins):
    tiles = []
    for pr in range(len(wins) // 2):
        ev, od = wins[2 * pr], wins[2 * pr + 1]
        tiles += [ev[:, :LANES], ev[:, LANES:] + od[:, :LANES], od[:, LANES:]]
    return jnp.concatenate(tiles, axis=1)


def _inproj_kernel(x_ref, ws5_ref, wml_ref, wssd_ref, mcw_ref, mcb_ref, scw_ref, scb_ref,
                   os5_ref, oml_ref, ossd_ref, mtail_ref, stail_ref, mbuf_ref, sbuf_ref,
                   *, tn, tiles_per_seq, ml_segs, ssd_segs):
    tm = x_ref.shape[0]

    @pl.when(pl.program_id(0) % tiles_per_seq == 0)
    def _():
        mtail_ref[...] = jnp.zeros(mtail_ref.shape, F32)
        stail_ref[...] = jnp.zeros(stail_ref.shape, F32)

    xb = x_ref[...].astype(BF16)
    hs5 = lax.dot_general(ws5_ref[...], xb, (((1,), (1,)), ((), ())), preferred_element_type=F32)
    half = os5_ref.shape[0] // 2
    os5_ref[0:half, :] = hs5[0:half, :]
    os5_ref[half:, :] = _silu(hs5[half:, :])

    def conv_silu(w_ref, o_ref, cw_ref, cb_ref, tail_ref, buf_ref, width):
        for j in range(0, width, tn):
            e = min(j + tn, width)
            r = jnp.dot(xb, w_ref[:, j:e], preferred_element_type=F32)
            buf_ref[0:TAIL, j:e] = tail_ref[:, j:e]
            buf_ref[TAIL:TAIL + tm, j:e] = r
            acc = cb_ref[:, j:e] + cw_ref[CONV_K - 1:CONV_K, j:e] * r
            for k in range(CONV_K - 1):
                off = TAIL - (CONV_K - 1) + k
                acc = acc + cw_ref[k:k + 1, j:e] * buf_ref[off:off + tm, j:e]
            tail_ref[:, j:e] = r[tm - TAIL:tm, :]
            o_ref[:, j:e] = _silu(acc)

    def plain(w_ref, o_ref, lo, hi, fn):
        for j in range(lo, hi, tn):
            e = min(j + tn, hi)
            r = jnp.dot(xb, w_ref[:, j:e], preferred_element_type=F32)
            o_ref[:, j:e] = r if fn is None else fn(r)

    conv_silu(wml_ref, oml_ref, mcw_ref, mcb_ref, mtail_ref, mbuf_ref, ml_segs[0][0])
    for lo, hi, fn in ml_segs[1:]:
        plain(wml_ref, oml_ref, lo, hi, fn)
    conv_silu(wssd_ref, ossd_ref, scw_ref, scb_ref, stail_ref, sbuf_ref, ssd_segs[0][0])
    for lo, hi, fn in ssd_segs[1:]:
        plain(wssd_ref, ossd_ref, lo, hi, fn)


def _inproj(x2, ws5t, wml, wssd, layer, mcw, mcb, scw, scb, tm, seq):
    t, k = x2.shape
    ns5, nml, nssd = ws5t.shape[1], wml.shape[2], wssd.shape[2]
    nqk, nconv = mcw.shape[1], scw.shape[1]
    wv = ML_HEADS * ML_DV
    wx = SSD_HEADS * SSD_HEADDIM
    ml_segs = ((nqk,), (nqk, nqk + wv, None), (nqk + wv, nqk + 2 * wv, jax.nn.sigmoid),
               (nqk + 2 * wv, nqk + 3 * wv, _silu))
    ssd_segs = ((nconv,), (nconv, nconv + wx, _silu), (nconv + wx, nssd, None))
    return pl.pallas_call(
        functools.partial(_inproj_kernel, tn=512, tiles_per_seq=seq // tm, ml_segs=ml_segs, ssd_segs=ssd_segs),
        grid=(t // tm,),
        in_specs=[pl.BlockSpec((tm, k), lambda i: (i, 0)),
                  _resident_layer(ws5t.shape, layer), _resident_layer(wml.shape, layer),
                  _resident_layer(wssd.shape, layer),
                  _resident(mcw.shape), _resident(mcb.shape), _resident(scw.shape), _resident(scb.shape)],
        out_specs=[pl.BlockSpec((ns5, tm), lambda i: (0, i)),
                   pl.BlockSpec((tm, nml), lambda i: (i, 0)),
                   pl.BlockSpec((tm, nssd), lambda i: (i, 0))],
        out_shape=[jax.ShapeDtypeStruct((ns5, t), F32),
                   jax.ShapeDtypeStruct((t, nml), F32),
                   jax.ShapeDtypeStruct((t, nssd), F32)],
        scratch_shapes=[pltpu.VMEM((TAIL, nqk), F32), pltpu.VMEM((TAIL, nconv), F32),
                        pltpu.VMEM((TAIL + tm, nqk), F32), pltpu.VMEM((TAIL + tm, nconv), F32)],
        compiler_params=pltpu.CompilerParams(dimension_semantics=("arbitrary",),
                                             vmem_limit_bytes=VMEM_LIMIT),
        name="inproj",
    )(x2, ws5t, wml, wssd, mcw, mcb, scw, scb)


def _outproj_kernel(x_ref, y_ref, u_ref, z_ref, b_ref, c_ref, d_ref, wg_ref, bg_ref,
                    wo_ref, g_ref, bb_ref, o_ref, *, alpha, sub):
    c0 = math.sqrt(2.0 / math.pi)
    na, nb = y_ref.shape[0], b_ref.shape[1]
    for r0 in range(0, x_ref.shape[0], sub):
        rows = slice(r0, r0 + sub)
        ys = y_ref[:, rows] + d_ref[...] * u_ref[:, rows]
        gl = 0.5 * ys * (1.0 + jnp.tanh(c0 * (ys + 0.044715 * (ys * ys * ys))))
        a = gl * jax.nn.sigmoid(_bdot(wg_ref[...], gl) + bg_ref[...]) * z_ref[:, rows]
        out = (_bdot_t0(a, wo_ref[0:na, :]) + _bdot(b_ref[rows, :], wo_ref[na:na + nb, :])
               + _bdot(c_ref[rows, :], wo_ref[na + nb:, :]))
        y = alpha * x_ref[rows, :] + out
        mu = jnp.mean(y, -1, keepdims=True)
        d = y - mu
        var = jnp.mean(d * d, -1, keepdims=True)
        o_ref[rows, :] = d * lax.rsqrt(var + EPS) * g_ref[...] + bb_ref[...]


def _outproj(x2, yt, ht, mb, mc, d5, wgt, bg5, wo, layer, g, b, tm, alpha):
    t, d = x2.shape
    wd = yt.shape[0]
    row = lambda i: (i, 0)
    return pl.pallas_call(
        functools.partial(_outproj_kernel, alpha=alpha, sub=min(OUT_SUB, tm)),
        grid=(t // tm,),
        in_specs=[pl.BlockSpec((tm, d), row),
                  pl.BlockSpec((wd, tm), lambda i: (0, i)),
                  pl.BlockSpec((wd, tm), lambda i: (0, i)),
                  pl.BlockSpec((wd, tm), lambda i: (1, i)),
                  pl.BlockSpec((tm, mb.shape[1]), row),
                  pl.BlockSpec((tm, mc.shape[1]), row),
                  _resident((wd, 1)), _resident((wd, wd)), _resident((wd, 1)),
                  _resident_layer(wo.shape, layer),
                  _resident((1, d)), _resident((1, d))],
        out_specs=pl.BlockSpec((tm, d), row),
        out_shape=jax.ShapeDtypeStruct((t, d), F32),
        compiler_params=pltpu.CompilerParams(dimension_semantics=("arbitrary",),
                                             vmem_limit_bytes=VMEM_LIMIT),
        name="outproj_ln",
    )(x2, yt, ht, ht, mb, mc, d5, wgt, bg5, wo, g, b)


def _s5_scan_kernel(u_ref, ptab_ref, stab_ref, ctab_ref, bbrow_ref, bbt_ref, rot_ref, y_ref,
                    tta_ref, ttb_ref, kk_ref, up_ref, acc_ref, sloc_ref, sin_ref, *, bsz, nchunk):
    q = S5_CHUNK
    p2 = 2 * S5_STATE
    ng = S5_GROUP
    srow = lax.broadcasted_iota(jnp.int32, (q, q), 0)
    tcol = lax.broadcasted_iota(jnp.int32, (q, q), 1)
    upper = tcol >= srow

    ctab = ctab_ref[0]

    def c_lam(co, a, b):
        return ctab[:, co:co + 1] * a + ctab[:, ng + co:ng + co + 1] * b

    cp0 = jnp.concatenate([c_lam(co, ptab_ref[0, 0], ptab_ref[0, 1]) for co in range(ng)], axis=1)
    vt = jnp.concatenate([c_lam(co, ptab_ref[0, 2], ptab_ref[0, 3]) for co in range(ng)], axis=1).astype(BF16)
    kk = jnp.dot(bbt_ref[0], cp0, precision=HI, preferred_element_type=F32)
    for co in range(ng):
        kk_ref[co] = kk[:, co * q:(co + 1) * q]

    u3 = u_ref[...].reshape(ng, bsz * nchunk, q)
    for pr in range(ng // 2):
        up_ref[pr] = jnp.concatenate([u3[2 * pr], u3[2 * pr + 1]], axis=1).astype(BF16)
    rs, ims = stab_ref[0, 0], stab_ref[0, 1]
    sloc = None
    for pr in range(ng // 2):
        wt = jnp.concatenate(
            [(rs * bbrow_ref[0, ci:ci + 1, :] + ims * bbrow_ref[0, ng + ci:ng + ci + 1, :]).astype(BF16)
             for ci in (2 * pr, 2 * pr + 1)], axis=0)
        part = jnp.dot(up_ref[pr], wt, preferred_element_type=F32)
        sloc = part if sloc is None else sloc + part
    sloc_ref[...] = sloc
    a1 = rot_ref[0, 0:1, :]
    a2 = rot_ref[0, 1:2, :]

    def body(c, states):
        nxt = []
        for b in range(bsz):
            r = pl.ds(b * nchunk + c, 1)
            s = states[b]
            sin_ref[r, :] = s
            swapped = jnp.concatenate([s[:, p2:], s[:, :p2]], axis=1)
            nxt.append(a1 * s + a2 * swapped + sloc_ref[r, :])
        return tuple(nxt)

    lax.fori_loop(0, nchunk, body, tuple(jnp.zeros((1, 2 * p2), F32) for _ in range(bsz)))
    acc_ref[...] = jnp.dot(sin_ref[:, 0:p2].astype(BF16), vt, preferred_element_type=F32)

    def build_pair(pr, t_ref):
        for k in range(2):
            for co in range(ng):
                lag = kk_ref[co, pl.ds(2 * pr + k, 1), :]
                blk = pltpu.roll(jnp.broadcast_to(lag, (q, q)), 0, 1, stride=1, stride_axis=0)
                t_ref[k * q:(k + 1) * q, co * q:(co + 1) * q] = jnp.where(upper, blk, 0.0).astype(BF16)

    def dot_pair(pr, t_ref):
        acc_ref[...] += jnp.dot(up_ref[pr], t_ref[...], preferred_element_type=F32)

    npair = ng // 2
    build_pair(0, tta_ref)

    def step(j, carry):
        build_pair(2 * j + 1, ttb_ref)
        dot_pair(2 * j, tta_ref)
        build_pair(2 * j + 2, tta_ref)
        dot_pair(2 * j + 1, ttb_ref)
        return carry

    lax.fori_loop(0, npair // 2 - 1, step, 0)
    build_pair(npair - 1, ttb_ref)
    dot_pair(npair - 2, tta_ref)
    dot_pair(npair - 1, ttb_ref)
    y = acc_ref[...]
    y_ref[...] = jnp.stack([y[:, co * q:(co + 1) * q] for co in range(ng)]).reshape(ng, bsz * nchunk * q)


def _s5_scan(u3, tabs, layer, bsz):
    ptab, stab, ctab, bbrow, bbt, rot = tabs
    q = S5_CHUNK
    w = u3.shape[0] // 2
    n = u3.shape[1] // q
    g = w // S5_GROUP
    qc = S5_GROUP * q
    p2 = 2 * S5_STATE
    def tab(a):
        zeros = (0,) * (a.ndim - 2)
        return pl.BlockSpec((None, 1) + tuple(a.shape[2:]), lambda i: (layer, i) + zeros)

    return pl.pallas_call(
        functools.partial(_s5_scan_kernel, bsz=bsz, nchunk=n // bsz),
        grid=(g,),
        in_specs=[pl.BlockSpec((S5_GROUP, n * q), lambda i: (i, 0)),
                  tab(ptab), tab(stab), tab(ctab), tab(bbrow), tab(bbt), tab(rot)],
        out_specs=pl.BlockSpec((S5_GROUP, n * q), lambda i: (i, 0)),
        out_shape=jax.ShapeDtypeStruct((w, n * q), F32),
        scratch_shapes=[pltpu.VMEM((2 * q, qc), BF16), pltpu.VMEM((2 * q, qc), BF16),
                        pltpu.VMEM((S5_GROUP, S5_GROUP, q), F32),
                        pltpu.VMEM((S5_GROUP // 2, n, 2 * q), BF16), pltpu.VMEM((n, qc), F32),
                        pltpu.VMEM((n, 2 * p2), F32), pltpu.VMEM((n, 2 * p2), F32)],
        compiler_params=pltpu.CompilerParams(dimension_semantics=("arbitrary",),
                                             vmem_limit_bytes=VMEM_LIMIT),
        name="s5_scan",
    )(u3, ptab, stab, ctab, bbrow, bbt, rot)


def _s5_tables(lam_re, lam_im, log_step, b_re, b_im, c_re, c_im):
    q = S5_CHUNK
    step = jnp.exp(log_step)[..., None]
    zr, zi = lam_re * step, lam_im * step
    er = jnp.exp(zr)
    nr, ni = er * jnp.cos(zi) - 1.0, er * jnp.sin(zi)
    den = lam_re * lam_re + lam_im * lam_im
    fr = (nr * lam_re + ni * lam_im) / den
    fi = (ni * lam_re - nr * lam_im) / den
    bbr = fr[..., None] * b_re - fi[..., None] * b_im
    bbi = fr[..., None] * b_im + fi[..., None] * b_re

    def powers(e, zre, zim):
        m = jnp.exp(e * zre)
        return m * jnp.cos(e * zim), m * jnp.sin(e * zim)

    lag = jnp.arange(q, dtype=F32)
    pr0, pi0 = powers(lag, zr[..., None], zi[..., None])
    pr1, pi1 = powers(lag + 1.0, zr[..., None], zi[..., None])
    cat = lambda a, b: jnp.concatenate([a, b], axis=2)
    ptab = jnp.stack([cat(pr0, -pi0), cat(-pi0, -pr0), cat(pr1, -pi1), cat(-pi1, -pr1)], axis=2)
    rs, ims = powers((q - 1.0 - lag)[:, None], zr[:, :, None, :], zi[:, :, None, :])
    stab = jnp.stack([jnp.concatenate([rs] * 4, -1), jnp.concatenate([-ims, ims, ims, -ims], -1)], axis=2)
    tr = lambda a: jnp.swapaxes(a, 2, 3)
    ctab = jnp.concatenate([tr(jnp.concatenate([c_re, c_re], -1)), tr(jnp.concatenate([c_im, c_im], -1))], -1)
    bbrow = jnp.concatenate([tr(jnp.concatenate([bbr, bbi, bbi, bbr], 2)),
                             tr(jnp.concatenate([bbi, bbr, bbr, bbi], 2))], 2)
    bbt = tr(jnp.concatenate([bbr, bbi], 2))
    prq, piq = powers(float(q), zr, zi)
    rot = jnp.stack([jnp.concatenate([prq] * 4, -1),
                     jnp.concatenate([-piq, piq, piq, -piq], -1)], axis=2)
    return ptab, stab, ctab, bbrow, bbt, rot


def _mlstm_kernel(h_ref, g_ref, gb_ref, ng_ref, o_ref, c_ref, m_ref, *, group):
    q = ML_CHUNK
    nqk = 2 * ML_HEADS * ML_DQK_PAD
    wv = ML_HEADS * ML_DV
    bsz = h_ref.shape[0]

    @pl.when(pl.program_id(0) == 0)
    def _():
        c_ref[...] = jnp.zeros(c_ref.shape, F32)
        m_ref[...] = jnp.zeros(m_ref.shape, F32)

    row = lax.broadcasted_iota(jnp.int32, (q, q), 0)
    col = lax.broadcasted_iota(jnp.int32, (q, q), 1)
    causal = col <= row
    tril = causal.astype(F32)
    lane = lax.broadcasted_iota(jnp.int32, (1, WIN), 1)
    one_b, zero_b = jnp.ones((), BF16), jnp.zeros((), BF16)

    gates, bc, gates_t, bc_t = [], [], [], []
    for b in range(bsz):
        g = g_ref[b] + gb_ref[...]
        c = jnp.dot(tril, -_softplus(-g), precision=HI, preferred_element_type=F32)
        gates.append(g)
        bc.append(c)
        gates_t.append(g.T)
        bc_t.append(c.T)

    hn = {}
    for par in (0, 1):
        allp = [(b, hd) for hd in range(par, ML_HEADS, 2) for b in range(bsz)]
        first_valid = _window(par)[1]
        vmask = (lane >= first_valid) if first_valid else (lane < WIN_VALID)
        nlane = 0 if first_valid else WIN_VALID
        for g0 in range(0, len(allp), group):
            pairs = allp[g0:g0 + group]
            idx = [b * ML_HEADS + hd for b, hd in pairs]
            n = len(pairs)

            def stack(fn):
                return jnp.stack([fn(b, hd) for b, hd in pairs])

            vext = [jnp.where(lane == nlane, one_b,
                              jnp.where(vmask, h_ref[b, :, nqk + _window(hd)[0]:nqk + _window(hd)[0] + WIN].astype(BF16),
                                        zero_b)) for b, hd in pairs]
            qh = [(h_ref[b, :, hd * ML_DQK_PAD:(hd + 1) * ML_DQK_PAD] * (ML_DQK ** -0.5)).astype(BF16)
                  for b, hd in pairs]
            kh = stack(lambda b, hd: h_ref[b, :, (ML_HEADS + hd) * ML_DQK_PAD:(ML_HEADS + hd + 1) * ML_DQK_PAD])
            bcol = stack(lambda b, hd: bc[b][:, ML_GATE_F + hd:ML_GATE_F + hd + 1])
            brow = stack(lambda b, hd: bc_t[b][ML_GATE_F + hd:ML_GATE_F + hd + 1, :])
            icol = stack(lambda b, hd: gates[b][:, ML_GATE_I + hd:ML_GATE_I + hd + 1])
            irow = stack(lambda b, hd: gates_t[b][ML_GATE_I + hd:ML_GATE_I + hd + 1, :])
            m_st = jnp.stack([m_ref[i, 0:1, 0:1] for i in idx])
            c_st = [c_ref[i] for i in idx]

            log_d = jnp.where(causal, bcol + (irow - brow), -jnp.inf)
            inter = bcol + m_st
            m_row = jnp.maximum(inter, jnp.max(log_d, -1, keepdims=True))
            dmat = jnp.exp(log_d - m_row)
            isc = jnp.exp(inter - m_row)
            s = (jnp.stack([_bdot_t1(qh[i], kh[i]) for i in range(n)]) * dmat).astype(BF16)
            num = (jnp.stack([jnp.dot(s[i], vext[i], preferred_element_type=F32) for i in range(n)])
                   + isc * jnp.stack([_bdot(qh[i], c_st[i]) for i in range(n)]))
            den = num[:, :, nlane:nlane + 1]
            hh = num * (1.0 / jnp.maximum(jnp.abs(den), jnp.exp(-m_row)))
            hv = jnp.where(lane == nlane, 0.0, hh)
            ms = jnp.sum(hv * hv, -1, keepdims=True) * (1.0 / ML_DV)
            hnorm = hv * lax.rsqrt(ms + EPS)

            blast = bcol[:, q - 1:q, :]
            log_w = blast - bcol + icol
            m_new = jnp.maximum(blast + m_st, jnp.max(log_w, 1, keepdims=True))
            kw = (kh * jnp.exp(log_w - m_new)).astype(BF16)
            decay = jnp.exp(blast + m_st - m_new)
            for i in range(n):
                c_ref[idx[i]] = decay[i] * c_st[i] + _bdot_t0(kw[i], vext[i])
                m_ref[idx[i]] = jnp.broadcast_to(m_new[i], m_ref.shape[1:])
                hn[pairs[i]] = hnorm[i]

    for b in range(bsz):
        joined = _join_windows([hn[(b, hd)] for hd in range(ML_HEADS)])
        o_sig = h_ref[b, :, nqk + wv:nqk + 2 * wv]
        z_act = h_ref[b, :, nqk + 2 * wv:nqk + 3 * wv]
        o_ref[b] = o_sig * (joined * ng_ref[...]) * z_act


def _mlstm_mixer(h3, hs3, gb, ng):
    bsz, seq, n = h3.shape
    wv = ML_HEADS * ML_DV
    blk = lambda c: (0, c, 0)
    gate_tile = hs3.shape[2] // LANES - 1
    return pl.pallas_call(
        functools.partial(_mlstm_kernel, group=ML_GROUP),
        grid=(seq // ML_CHUNK,),
        in_specs=[pl.BlockSpec((bsz, ML_CHUNK, n), blk),
                  pl.BlockSpec((bsz, ML_CHUNK, LANES), lambda c: (0, c, gate_tile)),
                  _resident((1, LANES)), _resident((1, wv))],
        out_specs=pl.BlockSpec((bsz, ML_CHUNK, wv), blk),
        out_shape=jax.ShapeDtypeStruct((bsz, seq, wv), F32),
        scratch_shapes=[pltpu.VMEM((bsz * ML_HEADS, ML_DQK_PAD, WIN), F32),
                        pltpu.VMEM((bsz * ML_HEADS, TAIL, LANES), F32)],
        compiler_params=pltpu.CompilerParams(dimension_semantics=("arbitrary",),
                                             vmem_limit_bytes=VMEM_LIMIT),
        name="mlstm",
    )(h3, hs3, gb, ng)


def _ssd_kernel(h_ref, dtb_ref, alog_ref, dsk_ref, ng_ref, o_ref, s_ref):
    q = SSD_CHUNK
    wx = SSD_HEADS * SSD_HEADDIM
    wbc = SSD_GROUPS * SSD_STATE
    nconv = wx + 2 * wbc
    bsz = h_ref.shape[0]

    @pl.when(pl.program_id(0) == 0)
    def _():
        s_ref[...] = jnp.zeros(s_ref.shape, F32)

    a = -jnp.exp(alog_ref[...])
    row = lax.broadcasted_iota(jnp.int32, (q, q), 0)
    col = lax.broadcasted_iota(jnp.int32, (q, q), 1)
    causal = col <= row
    tril = causal.astype(F32)
    lane = lax.broadcasted_iota(jnp.int32, (1, WIN), 1)
    lo_half = lane[:, :LANES] < SSD_HEADDIM
    zero_b = jnp.zeros((), BF16)

    def per_channel(hcols):
        return jnp.concatenate([jnp.where(lo_half, hcols[..., 2 * p:2 * p + 1], hcols[..., 2 * p + 1:2 * p + 2])
                                for p in range(SSD_HEADS // 2)], axis=-1)

    x = h_ref[:, :, 0:wx]
    dt = _softplus(h_ref[:, :, nconv + wx:nconv + wx + LANES] + dtb_ref[...])
    dta = dt * a
    acum = jnp.stack([jnp.dot(tril, dta[b], precision=HI, preferred_element_type=F32)
                      for b in range(bsz)])
    acum_t = [acum[b].T for b in range(bsz)]
    alast = acum[:, q - 1:q, :]
    eac_c = per_channel(jnp.exp(acum))
    dtx = x * per_channel(dt)
    dtx_dec = dtx * per_channel(jnp.exp(alast - acum))

    ywins = [[] for _ in range(bsz)]
    for g in range(SSD_GROUPS):
        w0, first_valid = _window(g)
        win = slice(w0, w0 + WIN)
        gmask = (lane >= first_valid) if first_valid else (lane < WIN_VALID)
        bm = [h_ref[b, :, wx + g * SSD_STATE:wx + (g + 1) * SSD_STATE].astype(BF16) for b in range(bsz)]
        cm = [h_ref[b, :, wx + wbc + g * SSD_STATE:wx + wbc + (g + 1) * SSD_STATE].astype(BF16) for b in range(bsz)]
        cbm = jnp.stack([_bdot_t1(cm[b], bm[b]) for b in range(bsz)])
        s_prev = [s_ref[b * SSD_GROUPS + g] for b in range(bsz)]
        xw = dtx[:, :, win].astype(BF16)
        xdec = jnp.where(gmask, dtx_dec[:, :, win].astype(BF16), zero_b)
        mmat = []
        for k in range(SSD_RATIO):
            hd = g * SSD_RATIO + k
            acol = acum[:, :, hd:hd + 1]
            arow = jnp.stack([acum_t[b][hd:hd + 1, :] for b in range(bsz)])
            mmat.append((cbm * jnp.exp(jnp.where(causal, acol - arow, -jnp.inf))).astype(BF16))
        yoff = jnp.stack([_bdot(cm[b], s_prev[b]) for b in range(bsz)]) * eac_c[:, :, win]
        for b in range(bsz):
            y = yoff[b]
            for k in range(SSD_RATIO):
                lo = first_valid + k * SSD_HEADDIM
                hmask = (lane >= lo) & (lane < lo + SSD_HEADDIM)
                y = y + jnp.dot(mmat[k][b], jnp.where(hmask, xw[b], zero_b), preferred_element_type=F32)
            ywins[b].append(y)
            s_ref[b * SSD_GROUPS + g] = s_prev[b] * eac_c[b, q - 1:q, win] + _bdot_t0(bm[b], xdec[b])

    for b in range(bsz):
        yz = (_join_windows(ywins[b]) + dsk_ref[...] * x[b]) * h_ref[b, :, nconv:nconv + wx]
        ms = jnp.mean(yz * yz, -1, keepdims=True)
        o_ref[b] = yz * lax.rsqrt(ms + EPS) * ng_ref[...]


def _ssd_mixer(h3, dtb, alog, dsk, ng):
    bsz, seq, n = h3.shape
    wx = SSD_HEADS * SSD_HEADDIM
    blk = lambda c: (0, c, 0)
    return pl.pallas_call(
        _ssd_kernel,
        grid=(seq // SSD_CHUNK,),
        in_specs=[pl.BlockSpec((bsz, SSD_CHUNK, n), blk),
                  _resident((1, LANES)), _resident((1, LANES)),
                  _resident((1, wx)), _resident((1, wx))],
        out_specs=pl.BlockSpec((bsz, SSD_CHUNK, wx), blk),
        out_shape=jax.ShapeDtypeStruct((bsz, seq, wx), F32),
        scratch_shapes=[pltpu.VMEM((bsz * SSD_GROUPS, SSD_STATE, WIN), F32)],
        compiler_params=pltpu.CompilerParams(dimension_semantics=("arbitrary",),
                                             vmem_limit_bytes=VMEM_LIMIT),
        name="ssd",
    )(h3, dtb, alog, dsk, ng)


def _pad_last(a, n):
    return jnp.pad(a, [(0, 0)] * (a.ndim - 1) + [(0, n - a.shape[-1])])


def _pad_heads(a, heads, width, padded):
    s = a.shape[:-1]
    a = a.reshape(s + (heads, width))
    return _pad_last(a, padded).reshape(s + (heads * padded,))


def kernel(x, w_in, w_out, ln_g, ln_b, s5_lambda_re, s5_lambda_im, s5_log_step, s5_b_re, s5_b_im, s5_c_re, s5_c_im, s5_d, s5_w_glu, s5_b_glu, ml_conv_w, ml_conv_b, ml_i_bias, ml_f_bias, ml_norm_g, ssd_conv_w, ssd_conv_b, ssd_dt_bias, ssd_a_log, ssd_d, ssd_norm_g):
    bsz, seq, dm = x.shape
    depth = w_in.shape[0]
    t = bsz * seq
    s5w = s5_d.shape[-1]
    mlw = ML_HEADS * ML_DV
    mlqk = ML_HEADS * ML_DQK
    ssdw = SSD_HEADS * SSD_HEADDIM
    ssdbc = SSD_GROUPS * SSD_STATE
    alpha = (2.0 * depth) ** 0.25
    tm = min(256, seq)

    sizes = (s5w, s5w, 2 * mlqk, mlw, ML_HEADS, ML_HEADS, mlw, mlw, ssdw + 2 * ssdbc, SSD_HEADS, ssdw)
    offs = [0]
    for sz in sizes:
        offs.append(offs[-1] + sz)
    col = lambda i: w_in[:, :, offs[i]:offs[i + 1]]
    w_s5t = jnp.swapaxes(w_in[:, :, :2 * s5w], 1, 2).astype(BF16)
    qkw = col(2)
    w_ml = jnp.concatenate([
        _pad_heads(qkw[..., :mlqk], ML_HEADS, ML_DQK, ML_DQK_PAD),
        _pad_heads(qkw[..., mlqk:], ML_HEADS, ML_DQK, ML_DQK_PAD),
        col(3), col(6), col(7)], -1).astype(BF16)
    w_ssd = jnp.concatenate([col(8), col(10),
                             _pad_last(jnp.concatenate([col(9), col(4), col(5)], -1), LANES)], -1).astype(BF16)

    ml_cw = jnp.concatenate([_pad_heads(ml_conv_w[..., :mlqk], ML_HEADS, ML_DQK, ML_DQK_PAD),
                             _pad_heads(ml_conv_w[..., mlqk:], ML_HEADS, ML_DQK, ML_DQK_PAD)], -1)
    ml_cb = jnp.concatenate([_pad_heads(ml_conv_b[..., :mlqk], ML_HEADS, ML_DQK, ML_DQK_PAD),
                             _pad_heads(ml_conv_b[..., mlqk:], ML_HEADS, ML_DQK, ML_DQK_PAD)], -1)
    ml_gb = _pad_last(jnp.concatenate([jnp.zeros_like(ssd_dt_bias), ml_i_bias, ml_f_bias], -1), LANES)
    ssd_dtb = _pad_last(ssd_dt_bias, LANES)
    ssd_al = _pad_last(ssd_a_log, LANES)
    ssd_dsk = jnp.repeat(ssd_d, SSD_HEADDIM, axis=-1)
    w_o = w_out.astype(BF16)
    s5_wgt = jnp.swapaxes(s5_w_glu, 1, 2).astype(BF16)
    s5_tabs = _s5_tables(s5_lambda_re, s5_lambda_im, s5_log_step, s5_b_re, s5_b_im, s5_c_re, s5_c_im)

    x2 = x.reshape(t, dm)
    for l in range(depth):
        ht_s5, h_ml, h_ssd = _inproj(x2, w_s5t, w_ml, w_ssd, l, ml_cw[l], ml_cb[l][None],
                                     ssd_conv_w[l], ssd_conv_b[l][None], tm, seq)
        yt_s5 = _s5_scan(ht_s5, s5_tabs, l, bsz)
        h_ssd3 = h_ssd.reshape(bsz, seq, -1)
        y_ml = _mlstm_mixer(h_ml.reshape(bsz, seq, -1), h_ssd3, ml_gb[l][None], ml_norm_g[l][None])
        y_ssd = _ssd_mixer(h_ssd3, ssd_dtb[l][None], ssd_al[l][None],
                           ssd_dsk[l][None], ssd_norm_g[l][None])
        x2 = _outproj(x2, yt_s5, ht_s5, y_ml.reshape(t, mlw), y_ssd.reshape(t, ssdw),
                      s5_d[l][:, None], s5_wgt[l], s5_b_glu[l][:, None],
                      w_o, l,
                      ln_g[l][None], ln_b[l][None], min(OUT_TILE, seq), alpha)
    return x2.reshape(bsz, seq, dm)
```

```python
import functools
import math

import jax
import jax.numpy as jnp
from jax import lax
from jax.experimental import pallas as pl
from jax.experimental.pallas import tpu as pltpu

F32 = jnp.float32
BF16 = jnp.bfloat16
HI = lax.Precision.HIGHEST

LANES = 128
TAIL = 8
VMEM_LIMIT = 56 * 1024 * 1024

CONV_K = 4
OUT_TILE = 512
OUT_SUB = 256
EPS = 1e-5

S5_GROUP = 16
S5_STATE = 64
S5_CHUNK = LANES

ML_HEADS = 4
ML_DV = 192
ML_DQK = 96
ML_DQK_PAD = 128
ML_CHUNK = 128
ML_GROUP = 8

SSD_HEADDIM = 64
SSD_HEADS = 12
SSD_GROUPS = 4
SSD_RATIO = SSD_HEADS // SSD_GROUPS
SSD_STATE = 128
SSD_CHUNK = 128

ML_GATE_I = SSD_HEADS
ML_GATE_F = SSD_HEADS + ML_HEADS
assert ML_CHUNK == SSD_CHUNK and ML_GATE_F + ML_HEADS <= LANES

WIN = 256
WIN_VALID = 192
assert ML_DV == WIN_VALID and SSD_RATIO * SSD_HEADDIM == WIN_VALID


def _silu(x):
    return x * jax.nn.sigmoid(x)


def _softplus(x):
    return jnp.maximum(x, 0.0) + jnp.log1p(jnp.exp(-jnp.abs(x)))


def _bdot(a, b):
    return jnp.dot(a.astype(BF16), b.astype(BF16), preferred_element_type=F32)


def _bdot_t0(a, b):
    return lax.dot_general(a.astype(BF16), b.astype(BF16), (((0,), (0,)), ((), ())),
                           preferred_element_type=F32)


def _bdot_t1(a, b):
    return lax.dot_general(a.astype(BF16), b.astype(BF16), (((1,), (1,)), ((), ())),
                           preferred_element_type=F32)


def _resident(shape):
    zeros = (0,) * len(shape)
    return pl.BlockSpec(shape, lambda *_: zeros, pipeline_mode=pl.Buffered(1))


def _resident_layer(shape, layer):
    idx = (layer,) + (0,) * (len(shape) - 1)
    return pl.BlockSpec((None,) + tuple(shape[1:]), lambda *_: idx, pipeline_mode=pl.Buffered(1))


def _window(idx):
    start = (idx // 2) * 2 * WIN_VALID + (LANES if idx % 2 else 0)
    return start, (WIN - WIN_VALID if idx % 2 else 0)


def _join_windows(wins):
    tiles = []
    for pr in range(len(wins) // 2):
        ev, od = wins[2 * pr], wins[2 * pr + 1]
        tiles += [ev[:, :LANES], ev[:, LANES:] + od[:, :LANES], od[:, LANES:]]
    return jnp.concatenate(tiles, axis=1)


def _inproj_kernel(x_ref, ws5_ref, wml_ref, wssd_ref, mcw_ref, mcb_ref, scw_ref, scb_ref,
                   os5_ref, oml_ref, ossd_ref, mtail_ref, stail_ref, mbuf_ref, sbuf_ref,
                   *, tn, tiles_per_seq, ml_segs, ssd_segs):
    tm = x_ref.shape[0]

    @pl.when(pl.program_id(0) % tiles_per_seq == 0)
    def _():
        mtail_ref[...] = jnp.zeros(mtail_ref.shape, F32)
        stail_ref[...] = jnp.zeros(stail_ref.shape, F32)

    xb = x_ref[...].astype(BF16)
    hs5 = lax.dot_general(ws5_ref[...], xb, (((1,), (1,)), ((), ())), preferred_element_type=F32)
    half = os5_ref.shape[0] // 2
    os5_ref[0:half, :] = hs5[0:half, :]
    os5_ref[half:, :] = _silu(hs5[half:, :])

    def conv_silu(w_ref, o_ref, cw_ref, cb_ref, tail_ref, buf_ref, width):
        for j in range(0, width, tn):
            e = min(j + tn, width)
            r = jnp.dot(xb, w_ref[:, j:e], preferred_element_type=F32)
            buf_ref[0:TAIL, j:e] = tail_ref[:, j:e]
            buf_ref[TAIL:TAIL + tm, j:e] = r
            acc = cb_ref[:, j:e] + cw_ref[CONV_K - 1:CONV_K, j:e] * r
            for k in range(CONV_K - 1):
                off = TAIL - (CONV_K - 1) + k
                acc = acc + cw_ref[k:k + 1, j:e] * buf_ref[off:off + tm, j:e]
            tail_ref[:, j:e] = r[tm - TAIL:tm, :]
            o_ref[:, j:e] = _silu(acc)

    def plain(w_ref, o_ref, lo, hi, fn):
        for j in range(lo, hi, tn):
            e = min(j + tn, hi)
            r = jnp.dot(xb, w_ref[:, j:e], preferred_element_type=F32)
            o_ref[:, j:e] = r if fn is None else fn(r)

    conv_silu(wml_ref, oml_ref, mcw_ref, mcb_ref, mtail_ref, mbuf_ref, ml_segs[0][0])
    for lo, hi, fn in ml_segs[1:]:
        plain(wml_ref, oml_ref, lo, hi, fn)
    conv_silu(wssd_ref, ossd_ref, scw_ref, scb_ref, stail_ref, sbuf_ref, ssd_segs[0][0])
    for lo, hi, fn in ssd_segs[1:]:
        plain(wssd_ref, ossd_ref, lo, hi, fn)


def _inproj(x2, ws5t, wml, wssd, layer, mcw, mcb, scw, scb, tm, seq):
    t, k = x2.shape
    ns5, nml, nssd = ws5t.shape[1], wml.shape[2], wssd.shape[2]
    nqk, nconv = mcw.shape[1], scw.shape[1]
    wv = ML_HEADS * ML_DV
    wx = SSD_HEADS * SSD_HEADDIM
    ml_segs = ((nqk,), (nqk, nqk + wv, None), (nqk + wv, nqk + 2 * wv, jax.nn.sigmoid),
               (nqk + 2 * wv, nqk + 3 * wv, _silu))
    ssd_segs = ((nconv,), (nconv, nconv + wx, _silu), (nconv + wx, nssd, None))
    return pl.pallas_call(
        functools.partial(_inproj_kernel, tn=512, tiles_per_seq=seq // tm, ml_segs=ml_segs, ssd_segs=ssd_segs),
        grid=(t // tm,),
        in_specs=[pl.BlockSpec((tm, k), lambda i: (i, 0)),
                  _resident_layer(ws5t.shape, layer), _resident_layer(wml.shape, layer),
                  _resident_layer(wssd.shape, layer),
                  _resident(mcw.shape), _resident(mcb.shape), _resident(scw.shape), _resident(scb.shape)],
        out_specs=[pl.BlockSpec((ns5, tm), lambda i: (0, i)),
                   pl.BlockSpec((tm, nml), lambda i: (i, 0)),
                   pl.BlockSpec((tm, nssd), lambda i: (i, 0))],
        out_shape=[jax.ShapeDtypeStruct((ns5, t), F32),
                   jax.ShapeDtypeStruct((t, nml), F32),
                   jax.ShapeDtypeStruct((t, nssd), F32)],
        scratch_shapes=[pltpu.VMEM((TAIL, nqk), F32), pltpu.VMEM((TAIL, nconv), F32),
                        pltpu.VMEM((TAIL + tm, nqk), F32), pltpu.VMEM((TAIL + tm, nconv), F32)],
        compiler_params=pltpu.CompilerParams(dimension_semantics=("arbitrary",),
                                             vmem_limit_bytes=VMEM_LIMIT),
        name="inproj",
    )(x2, ws5t, wml, wssd, mcw, mcb, scw, scb)


def _outproj_kernel(x_ref, y_ref, u_ref, z_ref, b_ref, c_ref, d_ref, wg_ref, bg_ref,
                    wo_ref, g_ref, bb_ref, o_ref, *, alpha, sub):
    c0 = math.sqrt(2.0 / math.pi)
    na, nb = y_ref.shape[0], b_ref.shape[1]
    for r0 in range(0, x_ref.shape[0], sub):
        rows = slice(r0, r0 + sub)
        ys = y_ref[:, rows] + d_ref[...] * u_ref[:, rows]
        gl = 0.5 * ys * (1.0 + jnp.tanh(c0 * (ys + 0.044715 * (ys * ys * ys))))
        a = gl * jax.nn.sigmoid(_bdot(wg_ref[...], gl) + bg_ref[...]) * z_ref[:, rows]
        out = (_bdot_t0(a, wo_ref[0:na, :]) + _bdot(b_ref[rows, :], wo_ref[na:na + nb, :])
               + _bdot(c_ref[rows, :], wo_ref[na + nb:, :]))
        y = alpha * x_ref[rows, :] + out
        mu = jnp.mean(y, -1, keepdims=True)
        d = y - mu
        var = jnp.mean(d * d, -1, keepdims=True)
        o_ref[rows, :] = d * lax.rsqrt(var + EPS) * g_ref[...] + bb_ref[...]


def _outproj(x2, yt, ht, mb, mc, d5, wgt, bg5, wo, layer, g, b, tm, alpha):
    t, d = x2.shape
    wd = yt.shape[0]
    row = lambda i: (i, 0)
    return pl.pallas_call(
        functools.partial(_outproj_kernel, alpha=alpha, sub=min(OUT_SUB, tm)),
        grid=(t // tm,),
        in_specs=[pl.BlockSpec((tm, d), row),
                  pl.BlockSpec((wd, tm), lambda i: (0, i)),
                  pl.BlockSpec((wd, tm), lambda i: (0, i)),
                  pl.BlockSpec((wd, tm), lambda i: (1, i)),
                  pl.BlockSpec((tm, mb.shape[1]), row),
                  pl.BlockSpec((tm, mc.shape[1]), row),
                  _resident((wd, 1)), _resident((wd, wd)), _resident((wd, 1)),
                  _resident_layer(wo.shape, layer),
                  _resident((1, d)), _resident((1, d))],
        out_specs=pl.BlockSpec((tm, d), row),
        out_shape=jax.ShapeDtypeStruct((t, d), F32),
        compiler_params=pltpu.CompilerParams(dimension_semantics=("arbitrary",),
                                             vmem_limit_bytes=VMEM_LIMIT),
        name="outproj_ln",
    )(x2, yt, ht, ht, mb, mc, d5, wgt, bg5, wo, g, b)


def _s5_scan_kernel(u_ref, ptab_ref, stab_ref, ctab_ref, bbrow_ref, bbt_ref, rot_ref, y_ref,
                    tta_ref, ttb_ref, kk_ref, up_ref, acc_ref, sloc_ref, sin_ref, *, bsz, nchunk):
    q = S5_CHUNK
    p2 = 2 * S5_STATE
    ng = S5_GROUP
    srow = lax.broadcasted_iota(jnp.int32, (q, q), 0)
    tcol = lax.broadcasted_iota(jnp.int32, (q, q), 1)
    upper = tcol >= srow

    ctab = ctab_ref[0]

    def c_lam(co, a, b):
        return ctab[:, co:co + 1] * a + ctab[:, ng + co:ng + co + 1] * b

    cp0 = jnp.concatenate([c_lam(co, ptab_ref[0, 0], ptab_ref[0, 1]) for co in range(ng)], axis=1)
    vt = jnp.concatenate([c_lam(co, ptab_ref[0, 2], ptab_ref[0, 3]) for co in range(ng)], axis=1).astype(BF16)
    kk = jnp.dot(bbt_ref[0], cp0, precision=HI, preferred_element_type=F32)
    for co in range(ng):
        kk_ref[co] = kk[:, co * q:(co + 1) * q]

    u3 = u_ref[...].reshape(ng, bsz * nchunk, q)
    for pr in range(ng // 2):
        up_ref[pr] = jnp.concatenate([u3[2 * pr], u3[2 * pr + 1]], axis=1).astype(BF16)
    rs, ims = stab_ref[0, 0], stab_ref[0, 1]
    sloc = None
    for pr in range(ng // 2):
        wt = jnp.concatenate(
            [(rs * bbrow_ref[0, ci:ci + 1, :] + ims * bbrow_ref[0, ng + ci:ng + ci + 1, :]).astype(BF16)
             for ci in (2 * pr, 2 * pr + 1)], axis=0)
        part = jnp.dot(up_ref[pr], wt, preferred_element_type=F32)
        sloc = part if sloc is None else sloc + part
    sloc_ref[...] = sloc
    a1 = rot_ref[0, 0:1, :]
    a2 = rot_ref[0, 1:2, :]

    def body(c, states):
        nxt = []
        for b in range(bsz):
            r = pl.ds(b * nchunk + c, 1)
            s = states[b]
            sin_ref[r, :] = s
            swapped = jnp.concatenate([s[:, p2:], s[:, :p2]], axis=1)
            nxt.append(a1 * s + a2 * swapped + sloc_ref[r, :])
        return tuple(nxt)

    lax.fori_loop(0, nchunk, body, tuple(jnp.zeros((1, 2 * p2), F32) for _ in range(bsz)))
    acc_ref[...] = jnp.dot(sin_ref[:, 0:p2].astype(BF16), vt, preferred_element_type=F32)

    def build_pair(pr, t_ref):
        for k in range(2):
            for co in range(ng):
                lag = kk_ref[co, pl.ds(2 * pr + k, 1), :]
                blk = pltpu.roll(jnp.broadcast_to(lag, (q, q)), 0, 1, stride=1, stride_axis=0)
                t_ref[k * q:(k + 1) * q, co * q:(co + 1) * q] = jnp.where(upper, blk, 0.0).astype(BF16)

    def dot_pair(pr, t_ref):
        acc_ref[...] += jnp.dot(up_ref[pr], t_ref[...], preferred_element_type=F32)

    npair = ng // 2
    build_pair(0, tta_ref)

    def step(j, carry):
        build_pair(2 * j + 1, ttb_ref)
        dot_pair(2 * j, tta_ref)
        build_pair(2 * j + 2, tta_ref)
        dot_pair(2 * j + 1, ttb_ref)
        return carry

    lax.fori_loop(0, npair // 2 - 1, step, 0)
    build_pair(npair - 1, ttb_ref)
    dot_pair(npair - 2, tta_ref)
    dot_pair(npair - 1, ttb_ref)
    y = acc_ref[...]
    y_ref[...] = jnp.stack([y[:, co * q:(co + 1) * q] for co in range(ng)]).reshape(ng, bsz * nchunk * q)


def _s5_scan(u3, tabs, layer, bsz):
    ptab, stab, ctab, bbrow, bbt, rot = tabs
    q = S5_CHUNK
    w = u3.shape[0] // 2
    n = u3.shape[1] // q
    g = w // S5_GROUP
    qc = S5_GROUP * q
    p2 = 2 * S5_STATE
    def tab(a):
        zeros = (0,) * (a.ndim - 2)
        return pl.BlockSpec((None, 1) + tuple(a.shape[2:]), lambda i: (layer, i) + zeros)

    return pl.pallas_call(
        functools.partial(_s5_scan_kernel, bsz=bsz, nchunk=n // bsz),
        grid=(g,),
        in_specs=[pl.BlockSpec((S5_GROUP, n * q), lambda i: (i, 0)),
                  tab(ptab), tab(stab), tab(ctab), tab(bbrow), tab(bbt), tab(rot)],
        out_specs=pl.BlockSpec((S5_GROUP, n * q), lambda i: (i, 0)),
        out_shape=jax.ShapeDtypeStruct((w, n * q), F32),
        scratch_shapes=[pltpu.VMEM((2 * q, qc), BF16), pltpu.VMEM((2 * q, qc), BF16),
                        pltpu.VMEM((S5_GROUP, S5_GROUP, q), F32),
                        pltpu.VMEM((S5_GROUP // 2, n, 2 * q), BF16), pltpu.VMEM((n, qc), F32),
                        pltpu.VMEM((n, 2 * p2), F32), pltpu.VMEM((n, 2 * p2), F32)],
        compiler_params=pltpu.CompilerParams(dimension_semantics=("arbitrary",),
                                             vmem_limit_bytes=VMEM_LIMIT),
        name="s5_scan",
    )(u3, ptab, stab, ctab, bbrow, bbt, rot)


def _s5_tables(lam_re, lam_im, log_step, b_re, b_im, c_re, c_im):
    q = S5_CHUNK
    step = jnp.exp(log_step)[..., None]
    zr, zi = lam_re * step, lam_im * step
    er = jnp.exp(zr)
    nr, ni = er * jnp.cos(zi) - 1.0, er * jnp.sin(zi)
    den = lam_re * lam_re + lam_im * lam_im
    fr = (nr * lam_re + ni * lam_im) / den
    fi = (ni * lam_re - nr * lam_im) / den
    bbr = fr[..., None] * b_re - fi[..., None] * b_im
    bbi = fr[..., None] * b_im + fi[..., None] * b_re

    def powers(e, zre, zim):
        m = jnp.exp(e * zre)
        return m * jnp.cos(e * zim), m * jnp.sin(e * zim)

    lag = jnp.arange(q, dtype=F32)
    pr0, pi0 = powers(lag, zr[..., None], zi[..., None])
    pr1, pi1 = powers(lag + 1.0, zr[..., None], zi[..., None])
    cat = lambda a, b: jnp.concatenate([a, b], axis=2)
    ptab = jnp.stack([cat(pr0, -pi0), cat(-pi0, -pr0), cat(pr1, -pi1), cat(-pi1, -pr1)], axis=2)
    rs, ims = powers((q - 1.0 - lag)[:, None], zr[:, :, None, :], zi[:, :, None, :])
    stab = jnp.stack([jnp.concatenate([rs] * 4, -1), jnp.concatenate([-ims, ims, ims, -ims], -1)], axis=2)
    tr = lambda a: jnp.swapaxes(a, 2, 3)
    ctab = jnp.concatenate([tr(jnp.concatenate([c_re, c_re], -1)), tr(jnp.concatenate([c_im, c_im], -1))], -1)
    bbrow = jnp.concatenate([tr(jnp.concatenate([bbr, bbi, bbi, bbr], 2)),
                             tr(jnp.concatenate([bbi, bbr, bbr, bbi], 2))], 2)
    bbt = tr(jnp.concatenate([bbr, bbi], 2))
    prq, piq = powers(float(q), zr, zi)
    rot = jnp.stack([jnp.concatenate([prq] * 4, -1),
                     jnp.concatenate([-piq, piq, piq, -piq], -1)], axis=2)
    return ptab, stab, ctab, bbrow, bbt, rot


def _mlstm_kernel(h_ref, g_ref, gb_ref, ng_ref, o_ref, c_ref, m_ref, *, group):
    q = ML_CHUNK
    nqk = 2 * ML_HEADS * ML_DQK_PAD
    wv = ML_HEADS * ML_DV
    bsz = h_ref.shape[0]

    @pl.when(pl.program_id(0) == 0)
    def _():
        c_ref[...] = jnp.zeros(c_ref.shape, F32)
        m_ref[...] = jnp.zeros(m_ref.shape, F32)

    row = lax.broadcasted_iota(jnp.int32, (q, q), 0)
    col = lax.broadcasted_iota(jnp.int32, (q, q), 1)
    causal = col <= row
    tril = causal.astype(F32)
    lane = lax.broadcasted_iota(jnp.int32, (1, WIN), 1)
    one_b, zero_b = jnp.ones((), BF16), jnp.zeros((), BF16)

    gates, bc, gates_t, bc_t = [], [], [], []
    for b in range(bsz):
        g = g_ref[b] + gb_ref[...]
        c = jnp.dot(tril, -_softplus(-g), precision=HI, preferred_element_type=F32)
        gates.append(g)
        bc.append(c)
        gates_t.append(g.T)
        bc_t.append(c.T)

    hn = {}
    for par in (0, 1):
        allp = [(b, hd) for hd in range(par, ML_HEADS, 2) for b in range(bsz)]
        first_valid = _window(par)[1]
        vmask = (lane >= first_valid) if first_valid else (lane < WIN_VALID)
        nlane = 0 if first_valid else WIN_VALID
        for g0 in range(0, len(allp), group):
            pairs = allp[g0:g0 + group]
            idx = [b * ML_HEADS + hd for b, hd in pairs]
            n = len(pairs)

            def stack(fn):
                return jnp.stack([fn(b, hd) for b, hd in pairs])

            vext = [jnp.where(lane == nlane, one_b,
                              jnp.where(vmask, h_ref[b, :, nqk + _window(hd)[0]:nqk + _window(hd)[0] + WIN].astype(BF16),
                                        zero_b)) for b, hd in pairs]
            qh = [(h_ref[b, :, hd * ML_DQK_PAD:(hd + 1) * ML_DQK_PAD] * (ML_DQK ** -0.5)).astype(BF16)
                  for b, hd in pairs]
            kh = stack(lambda b, hd: h_ref[b, :, (ML_HEADS + hd) * ML_DQK_PAD:(ML_HEADS + hd + 1) * ML_DQK_PAD])
            bcol = stack(lambda b, hd: bc[b][:, ML_GATE_F + hd:ML_GATE_F + hd + 1])
            brow = stack(lambda b, hd: bc_t[b][ML_GATE_F + hd:ML_GATE_F + hd + 1, :])
            icol = stack(lambda b, hd: gates[b][:, ML_GATE_I + hd:ML_GATE_I + hd + 1])
            irow = stack(lambda b, hd: gates_t[b][ML_GATE_I + hd:ML_GATE_I + hd + 1, :])
            m_st = jnp.stack([m_ref[i, 0:1, 0:1] for i in idx])
            c_st = [c_ref[i] for i in idx]

            log_d = jnp.where(causal, bcol + (irow - brow), -jnp.inf)
            inter = bcol + m_st
            m_row = jnp.maximum(inter, jnp.max(log_d, -1, keepdims=True))
            dmat = jnp.exp(log_d - m_row)
            isc = jnp.exp(inter - m_row)
            s = (jnp.stack([_bdot_t1(qh[i], kh[i]) for i in range(n)]) * dmat).astype(BF16)
            num = (jnp.stack([jnp.dot(s[i], vext[i], preferred_element_type=F32) for i in range(n)])
                   + isc * jnp.stack([_bdot(qh[i], c_st[i]) for i in range(n)]))
            den = num[:, :, nlane:nlane + 1]
            hh = num * (1.0 / jnp.maximum(jnp.abs(den), jnp.exp(-m_row)))
            hv = jnp.where(lane == nlane, 0.0, hh)
            ms = jnp.sum(hv * hv, -1, keepdims=True) * (1.0 / ML_DV)
            hnorm = hv * lax.rsqrt(ms + EPS)

            blast = bcol[:, q - 1:q, :]
            log_w = blast - bcol + icol
            m_new = jnp.maximum(blast + m_st, jnp.max(log_w, 1, keepdims=True))
            kw = (kh * jnp.exp(log_w - m_new)).astype(BF16)
            decay = jnp.exp(blast + m_st - m_new)
            for i in range(n):
                c_ref[idx[i]] = decay[i] * c_st[i] + _bdot_t0(kw[i], vext[i])
                m_ref[idx[i]] = jnp.broadcast_to(m_new[i], m_ref.shape[1:])
                hn[pairs[i]] = hnorm[i]

    for b in range(bsz):
        joined = _join_windows([hn[(b, hd)] for hd in range(ML_HEADS)])
        o_sig = h_ref[b, :, nqk + wv:nqk + 2 * wv]
        z_act = h_ref[b, :, nqk + 2 * wv:nqk + 3 * wv]
        o_ref[b] = o_sig * (joined * ng_ref[...]) * z_act


def _mlstm_mixer(h3, hs3, gb, ng):
    bsz, seq, n = h3.shape
    wv = ML_HEADS * ML_DV
    blk = lambda c: (0, c, 0)
    gate_tile = hs3.shape[2] // LANES - 1
    return pl.pallas_call(
        functools.partial(_mlstm_kernel, group=ML_GROUP),
        grid=(seq // ML_CHUNK,),
        in_specs=[pl.BlockSpec((bsz, ML_CHUNK, n), blk),
                  pl.BlockSpec((bsz, ML_CHUNK, LANES), lambda c: (0, c, gate_tile)),
                  _resident((1, LANES)), _resident((1, wv))],
        out_specs=pl.BlockSpec((bsz, ML_CHUNK, wv), blk),
        out_shape=jax.ShapeDtypeStruct((bsz, seq, wv), F32),
        scratch_shapes=[pltpu.VMEM((bsz * ML_HEADS, ML_DQK_PAD, WIN), F32),
                        pltpu.VMEM((bsz * ML_HEADS, TAIL, LANES), F32)],
        compiler_params=pltpu.CompilerParams(dimension_semantics=("arbitrary",),
                                             vmem_limit_bytes=VMEM_LIMIT),
        name="mlstm",
    )(h3, hs3, gb, ng)


def _ssd_kernel(h_ref, dtb_ref, alog_ref, dsk_ref, ng_ref, o_ref, s_ref):
    q = SSD_CHUNK
    wx = SSD_HEADS * SSD_HEADDIM
    wbc = SSD_GROUPS * SSD_STATE
    nconv = wx + 2 * wbc
    bsz = h_ref.shape[0]

    @pl.when(pl.program_id(0) == 0)
    def _():
        s_ref[...] = jnp.zeros(s_ref.shape, F32)

    a = -jnp.exp(alog_ref[...])
    row = lax.broadcasted_iota(jnp.int32, (q, q), 0)
    col = lax.broadcasted_iota(jnp.int32, (q, q), 1)
    causal = col <= row
    tril = causal.astype(F32)
    lane = lax.broadcasted_iota(jnp.int32, (1, WIN), 1)
    lo_half = lane[:, :LANES] < SSD_HEADDIM
    zero_b = jnp.zeros((), BF16)

    def per_channel(hcols):
        return jnp.concatenate([jnp.where(lo_half, hcols[..., 2 * p:2 * p + 1], hcols[..., 2 * p + 1:2 * p + 2])
                                for p in range(SSD_HEADS // 2)], axis=-1)

    x = h_ref[:, :, 0:wx]
    dt = _softplus(h_ref[:, :, nconv + wx:nconv + wx + LANES] + dtb_ref[...])
    dta = dt * a
    acum = jnp.stack([jnp.dot(tril, dta[b], precision=HI, preferred_element_type=F32)
                      for b in range(bsz)])
    acum_t = [acum[b].T for b in range(bsz)]
    alast = acum[:, q - 1:q, :]
    eac_c = per_channel(jnp.exp(acum))
    dtx = x * per_channel(dt)
    dtx_dec = dtx * per_channel(jnp.exp(alast - acum))

    ywins = [[] for _ in range(bsz)]
    for g in range(SSD_GROUPS):
        w0, first_valid = _window(g)
        win = slice(w0, w0 + WIN)
        gmask = (lane >= first_valid) if first_valid else (lane < WIN_VALID)
        bm = [h_ref[b, :, wx + g * SSD_STATE:wx + (g + 1) * SSD_STATE].astype(BF16) for b in range(bsz)]
        cm = [h_ref[b, :, wx + wbc + g * SSD_STATE:wx + wbc + (g + 1) * SSD_STATE].astype(BF16) for b in range(bsz)]
        cbm = jnp.stack([_bdot_t1(cm[b], bm[b]) for b in range(bsz)])
        s_prev = [s_ref[b * SSD_GROUPS + g] for b in range(bsz)]
        xw = dtx[:, :, win].astype(BF16)
        xdec = jnp.where(gmask, dtx_dec[:, :, win].astype(BF16), zero_b)
        mmat = []
        for k in range(SSD_RATIO):
            hd = g * SSD_RATIO + k
            acol = acum[:, :, hd:hd + 1]
            arow = jnp.stack([acum_t[b][hd:hd + 1, :] for b in range(bsz)])
            mmat.append((cbm * jnp.exp(jnp.where(causal, acol - arow, -jnp.inf))).astype(BF16))
        yoff = jnp.stack([_bdot(cm[b], s_prev[b]) for b in range(bsz)]) * eac_c[:, :, win]
        for b in range(bsz):
            y = yoff[b]
            for k in range(SSD_RATIO):
                lo = first_valid + k * SSD_HEADDIM
                hmask = (lane >= lo) & (lane < lo + SSD_HEADDIM)
                y = y + jnp.dot(mmat[k][b], jnp.where(hmask, xw[b], zero_b), preferred_element_type=F32)
            ywins[b].append(y)
            s_ref[b * SSD_GROUPS + g] = s_prev[b] * eac_c[b, q - 1:q, win] + _bdot_t0(bm[b], xdec[b])

    for b in range(bsz):
        yz = (_join_windows(ywins[b]) + dsk_ref[...] * x[b]) * h_ref[b, :, nconv:nconv + wx]
        ms = jnp.mean(yz * yz, -1, keepdims=True)
        o_ref[b] = yz * lax.rsqrt(ms + EPS) * ng_ref[...]


def _ssd_mixer(h3, dtb, alog, dsk, ng):
    bsz, seq, n = h3.shape
    wx = SSD_HEADS * SSD_HEADDIM
    blk = lambda c: (0, c, 0)
    return pl.pallas_call(
        _ssd_kernel,
        grid=(seq // SSD_CHUNK,),
        in_specs=[pl.BlockSpec((bsz, SSD_CHUNK, n), blk),
                  _resident((1, LANES)), _resident((1, LANES)),
                  _resident((1, wx)), _resident((1, wx))],
        out_specs=pl.BlockSpec((bsz, SSD_CHUNK, wx), blk),
        out_shape=jax.ShapeDtypeStruct((bsz, seq, wx), F32),
        scratch_shapes=[pltpu.VMEM((bsz * SSD_GROUPS, SSD_STATE, WIN), F32)],
        compiler_params=pltpu.CompilerParams(dimension_semantics=("arbitrary",),
                                             vmem_limit_bytes=VMEM_LIMIT),
        name="ssd",
    )(h3, dtb, alog, dsk, ng)


def _cast_kernel(w_ref, o_ref):
    o_ref[...] = w_ref[...].astype(BF16)


def _to_bf16(w, rows):
    d, k, n = w.shape
    blk = pl.BlockSpec((1, rows, n), lambda i, j: (i, j, 0))
    return pl.pallas_call(
        _cast_kernel,
        grid=(d, k // rows),
        in_specs=[blk],
        out_specs=blk,
        out_shape=jax.ShapeDtypeStruct(w.shape, BF16),
        compiler_params=pltpu.CompilerParams(dimension_semantics=("arbitrary", "arbitrary"),
                                             vmem_limit_bytes=VMEM_LIMIT),
        name="cast_bf16",
    )(w)


def _pad_last(a, n):
    return jnp.pad(a, [(0, 0)] * (a.ndim - 1) + [(0, n - a.shape[-1])])


def _pad_heads(a, heads, width, padded):
    s = a.shape[:-1]
    a = a.reshape(s + (heads, width))
    return _pad_last(a, padded).reshape(s + (heads * padded,))


def kernel(x, w_in, w_out, ln_g, ln_b, s5_lambda_re, s5_lambda_im, s5_log_step, s5_b_re, s5_b_im, s5_c_re, s5_c_im, s5_d, s5_w_glu, s5_b_glu, ml_conv_w, ml_conv_b, ml_i_bias, ml_f_bias, ml_norm_g, ssd_conv_w, ssd_conv_b, ssd_dt_bias, ssd_a_log, ssd_d, ssd_norm_g):
    bsz, seq, dm = x.shape
    depth = w_in.shape[0]
    t = bsz * seq
    s5w = s5_d.shape[-1]
    mlw = ML_HEADS * ML_DV
    mlqk = ML_HEADS * ML_DQK
    ssdw = SSD_HEADS * SSD_HEADDIM
    ssdbc = SSD_GROUPS * SSD_STATE
    alpha = (2.0 * depth) ** 0.25
    tm = min(256, seq)

    sizes = (s5w, s5w, 2 * mlqk, mlw, ML_HEADS, ML_HEADS, mlw, mlw, ssdw + 2 * ssdbc, SSD_HEADS, ssdw)
    offs = [0]
    for sz in sizes:
        offs.append(offs[-1] + sz)
    w_in_b = _to_bf16(w_in, min(256, dm))
    col = lambda i: w_in_b[:, :, offs[i]:offs[i + 1]]
    w_s5t = jnp.swapaxes(w_in_b[:, :, :2 * s5w], 1, 2)
    qkw = col(2)
    w_ml = jnp.concatenate([
        _pad_heads(qkw[..., :mlqk], ML_HEADS, ML_DQK, ML_DQK_PAD),
        _pad_heads(qkw[..., mlqk:], ML_HEADS, ML_DQK, ML_DQK_PAD),
        col(3), col(6), col(7)], -1).astype(BF16)
    w_ssd = jnp.concatenate([col(8), col(10),
                             _pad_last(jnp.concatenate([col(9), col(4), col(5)], -1), LANES)], -1).astype(BF16)

    ml_cw = jnp.concatenate([_pad_heads(ml_conv_w[..., :mlqk], ML_HEADS, ML_DQK, ML_DQK_PAD),
                             _pad_heads(ml_conv_w[..., mlqk:], ML_HEADS, ML_DQK, ML_DQK_PAD)], -1)
    ml_cb = jnp.concatenate([_pad_heads(ml_conv_b[..., :mlqk], ML_HEADS, ML_DQK, ML_DQK_PAD),
                             _pad_heads(ml_conv_b[..., mlqk:], ML_HEADS, ML_DQK, ML_DQK_PAD)], -1)
    ml_gb = _pad_last(jnp.concatenate([jnp.zeros_like(ssd_dt_bias), ml_i_bias, ml_f_bias], -1), LANES)
    ssd_dtb = _pad_last(ssd_dt_bias, LANES)
    ssd_al = _pad_last(ssd_a_log, LANES)
    ssd_dsk = jnp.repeat(ssd_d, SSD_HEADDIM, axis=-1)
    w_o = w_out.astype(BF16)
    s5_wgt = jnp.swapaxes(s5_w_glu, 1, 2).astype(BF16)
    s5_tabs = _s5_tables(s5_lambda_re, s5_lambda_im, s5_log_step, s5_b_re, s5_b_im, s5_c_re, s5_c_im)

    x2 = x.reshape(t, dm)
    for l in range(depth):
        ht_s5, h_ml, h_ssd = _inproj(x2, w_s5t, w_ml, w_ssd, l, ml_cw[l], ml_cb[l][None],
                                     ssd_conv_w[l], ssd_conv_b[l][None], tm, seq)
        yt_s5 = _s5_scan(ht_s5, s5_tabs, l, bsz)
        h_ssd3 = h_ssd.reshape(bsz, seq, -1)
        y_ml = _mlstm_mixer(h_ml.reshape(bsz, seq, -1), h_ssd3, ml_gb[l][None], ml_norm_g[l][None])
        y_ssd = _ssd_mixer(h_ssd3, ssd_dtb[l][None], ssd_al[l][None],
                           ssd_dsk[l][None], ssd_norm_g[l][None])
        x2 = _outproj(x2, yt_s5, ht_s5, y_ml.reshape(t, mlw), y_ssd.reshape(t, ssdw),
                      s5_d[l][:, None], s5_wgt[l], s5_b_glu[l][:, None],
                      w_o, l,
                      ln_g[l][None], ln_b[l][None], min(OUT_TILE, seq), alpha)
    return x2.reshape(bsz, seq, dm)
```
